```python
import jax
import jax.numpy as jnp
from jax import lax
import numpy as np

D_MODEL = 1024
BATCH = 8
SEQ = 4096
DEPTH = 2

GRID_W = 64
CTX_LEN = 256
D_FF = 4 * D_MODEL
D_MIX = D_MODEL
CHUNK = 64
Q_BLOCK = 128
EPS = 1e-6
ROPE_THETA = 10000.0

GLA_H = 8
GLA_DK = D_MIX // (4 * GLA_H)
GLA_DV = D_MIX // (2 * GLA_H)
GLA_LR = 16
GLA_TAU = 16.0
ATT_H = 8
ATT_KV = 2
ATT_HD = D_MIX // (2 * ATT_H)
ML_H = 8
ML_HD = D_MIX // (2 * ML_H)
ML_D = ML_H * ML_HD
CONV_W = 3
RET_H = 8
RET_HD = D_MIX // (2 * RET_H)
RET_D = RET_H * RET_HD

AB_SIZES = (GLA_H * GLA_DK, GLA_H * GLA_DK, GLA_H * GLA_DV, GLA_H * GLA_DV, GLA_LR, GLA_LR, ATT_H * ATT_HD, ATT_KV * ATT_HD, ATT_KV * ATT_HD)
AB_IN = sum(AB_SIZES)
CD_SIZES = (ML_D, ML_D, RET_D, RET_D, RET_D, RET_D)
CD_IN = sum(CD_SIZES)
N_EVEN = (DEPTH + 1) // 2
N_ODD = DEPTH // 2

kernel_name = 'hybrid_gla_gqa_mlstm_retention_dit'


def rmsnorm(x, g):
    xf = x.astype(jnp.float32)
    y = xf * lax.rsqrt(jnp.mean(xf * xf, axis=-1, keepdims=True) + EPS)
    return (y * g.astype(jnp.float32)).astype(x.dtype)


def modulate(x, g, shift, scale):
    return rmsnorm(x, g) * (1.0 + scale) + shift


def sq_relu_mlp(h, w1, w2):
    return jnp.square(jax.nn.relu(h @ w1)) @ w2


def heads(t, n_heads):
    b, l, _ = t.shape
    return t.reshape(b, l, n_heads, -1).transpose(0, 2, 1, 3)


def merge(t):
    b, h, l, d = t.shape
    return t.transpose(0, 2, 1, 3).reshape(b, l, h * d)


def split_cols(u, sizes):
    return jnp.split(u, np.cumsum(sizes)[:-1].tolist(), axis=-1)


def flip_seq(t):
    return jnp.flip(t, axis=2)


def grid_angles(n_tokens, head_dim):
    n_rows = n_tokens // GRID_W
    rows, cols = jnp.meshgrid(jnp.arange(n_rows), jnp.arange(GRID_W), indexing='ij')
    axis_dim = head_dim // 2
    inv_freq = ROPE_THETA ** (-jnp.arange(0, axis_dim, 2, dtype=jnp.float32) / axis_dim)
    ang_r = rows.reshape(-1, 1).astype(jnp.float32) * inv_freq
    ang_c = cols.reshape(-1, 1).astype(jnp.float32) * inv_freq
    return ang_r, ang_c


def rope_axis(x, ang):
    x1, x2 = jnp.split(x, 2, axis=-1)
    cos, sin = jnp.cos(ang), jnp.sin(ang)
    return jnp.concatenate([x1 * cos - x2 * sin, x1 * sin + x2 * cos], axis=-1)


def axial_rope(x, ang_r, ang_c):
    xf = x.astype(jnp.float32)
    half = x.shape[-1] // 2
    y = jnp.concatenate([rope_axis(xf[..., :half], ang_r), rope_axis(xf[..., half:], ang_c)], axis=-1)
    return y.astype(x.dtype)


def short_conv(x, w, b):
    y = lax.conv_general_dilated(x, w[:, None, :], window_strides=(1,), padding=[(CONV_W // 2, CONV_W // 2)],
                                 dimension_numbers=('NWC', 'WIO', 'NWC'), feature_group_count=x.shape[-1])
    return y + b


def gla_dir(q, k, v, log_g, state0, want_out):
    bsz, nh, length, dk = q.shape
    dv = v.shape[-1]
    n = length // CHUNK
    q, k, v, lg = (t.reshape(bsz, nh, n, CHUNK, t.shape[-1]) for t in (q, k, v, log_g))
    b = jnp.cumsum(lg.astype(jnp.float32), axis=3)
    b_end = b[:, :, :, -1:, :]
    u = jnp.einsum('bhncd,bhnce->bhnde', k * jnp.exp(b_end - b), v)
    decay = jnp.exp(b_end[:, :, :, 0, :])

    def step(s, inp):
        dec, uc = inp
        return dec[..., None] * s + uc, s

    s_final, s_start = lax.scan(step, state0, (jnp.moveaxis(decay, 2, 0), jnp.moveaxis(u, 2, 0)))
    if not want_out:
        return None, s_final
    s_start = jnp.moveaxis(s_start, 0, 2)
    qd = q * jnp.exp(b)
    ki = k * jnp.exp(-b)
    tril = jnp.tril(jnp.ones((CHUNK, CHUNK), dtype=bool))
    a = jnp.where(tril, jnp.einsum('bhnid,bhnjd->bhnij', qd, ki), 0.0)
    o = jnp.einsum('bhnij,bhnje->bhnie', a, v) + jnp.einsum('bhnid,bhnde->bhnie', qd, s_start)
    return o.reshape(bsz, nh, length, dv).astype(v.dtype), s_final


def mlstm_dir(q, k, v, i_pre, logf, state0, want_out):
    bsz, nh, length, d = q.shape
    n = length // CHUNK
    q, k, v = (t.reshape(bsz, nh, n, CHUNK, d) for t in (q, k, v))
    ig = i_pre.reshape(bsz, nh, n, CHUNK).astype(jnp.float32)
    b = jnp.cumsum(logf.reshape(bsz, nh, n, CHUNK).astype(jnp.float32), axis=-1)
    b_end = b[..., -1]
    a = b_end[..., None] - b + ig
    m_loc = jnp.max(a, axis=-1)
    w = jnp.exp(a - m_loc[..., None])
    u_c = jnp.einsum('bhnc,bhncd,bhnce->bhnde', w, k, v)
    u_n = jnp.einsum('bhnc,bhncd->bhnd', w, k)

    def step(carry, inp):
        c_s, n_s, m_s = carry
        be, ml, uc, un = inp
        m_new = jnp.maximum(be + m_s, ml)
        s_old = jnp.exp(be + m_s - m_new)
        s_new = jnp.exp(ml - m_new)
        new = (s_old[..., None, None] * c_s + s_new[..., None, None] * uc,
               s_old[..., None] * n_s + s_new[..., None] * un, m_new)
        return new, carry

    final, starts = lax.scan(step, state0, tuple(jnp.moveaxis(t, 2, 0) for t in (b_end, m_loc, u_c, u_n)))
    if not want_out:
        return None, final
    c_st, n_st, m_st = (jnp.moveaxis(t, 0, 2) for t in starts)
    tril = jnp.tril(jnp.ones((CHUNK, CHUNK), dtype=bool))
    dmat = jnp.where(tril, b[..., :, None] - b[..., None, :] + ig[..., None, :], -jnp.inf)
    m_inter = b + m_st[..., None]
    m_i = jnp.maximum(m_inter, jnp.max(dmat, axis=-1))
    qk = jnp.einsum('bhnid,bhnjd->bhnij', q, k) * jnp.exp(dmat - m_i[..., None])
    s_inter = jnp.exp(m_inter - m_i)
    num = jnp.einsum('bhnij,bhnje->bhnie', qk, v) + s_inter[..., None] * jnp.einsum('bhnid,bhnde->bhnie', q, c_st)
    den = jnp.sum(qk, axis=-1) + s_inter * jnp.einsum('bhnid,bhnd->bhni', q, n_st)
    h = num / jnp.maximum(jnp.abs(den), jnp.exp(-m_i))[..., None]
    return h.reshape(bsz, nh, length, d).astype(v.dtype), final


def bidir_two_stream(scan_dir, lat_fwd, lat_bwd, ctx_fwd, ctx_bwd, state0, ctx_out):
    ctx_of, st_f = scan_dir(*ctx_fwd, state0, ctx_out)
    lat_of, _ = scan_dir(*lat_fwd, st_f, True)
    ctx_ob, st_b = scan_dir(*[flip_seq(t) for t in ctx_bwd], state0, ctx_out)
    lat_ob, _ = scan_dir(*[flip_seq(t) for t in lat_bwd], st_b, True)
    lat = lat_of + flip_seq(lat_ob)
    ctx = ctx_of + flip_seq(ctx_ob) if ctx_out else None
    return lat, ctx


def softmax_attend(q, k, v):
    s = jnp.einsum('bkgqd,bksd->bkgqs', q, k).astype(jnp.float32) * (q.shape[-1] ** -0.5)
    p = jax.nn.softmax(s, axis=-1).astype(v.dtype)
    return jnp.einsum('bkgqs,bksd->bkgqd', p, v)


def blocked_attend(q, k, v):
    bsz, nkv, g, length, d = q.shape
    nb = length // Q_BLOCK
    qb = jnp.moveaxis(q.reshape(bsz, nkv, g, nb, Q_BLOCK, d), 3, 0)
    ob = lax.map(lambda blk: softmax_attend(blk, k, v), qb)
    return jnp.moveaxis(ob, 0, 3).reshape(bsz, nkv, g, length, d)


def group_q(q):
    b, h, l, d = q.shape
    return q.reshape(b, ATT_KV, h // ATT_KV, l, d)


def mix_ab(ul, uc, w_gate, b_gate, gla_g, qk_g, ctx_out):
    lat, cx = split_cols(ul, AB_SIZES), split_cols(uc, AB_SIZES)
    n_lat = ul.shape[1]

    def gla_streams(p):
        q = heads(p[0], GLA_H) * GLA_DK ** -0.5
        k, v = heads(p[1], GLA_H), heads(p[2], GLA_H)
        lg_f, lg_b = (heads(jax.nn.log_sigmoid((lr @ w_gate[d] + b_gate[d]).astype(jnp.float32)) / GLA_TAU, GLA_H)
                      for d, lr in enumerate((p[4], p[5])))
        return (q, k, v, lg_f), (q, k, v, lg_b)

    lat_f, lat_b = gla_streams(lat)
    ctx_f, ctx_b = gla_streams(cx)
    s0 = jnp.zeros((ul.shape[0], GLA_H, GLA_DK, GLA_DV), jnp.float32)
    o_lat, o_ctx = bidir_two_stream(gla_dir, lat_f, lat_b, ctx_f, ctx_b, s0, ctx_out)

    def gla_out(o, g):
        return merge(rmsnorm(o, gla_g)) * jax.nn.silu(g)

    ang_r, ang_c = grid_angles(n_lat, ATT_HD)

    def q_of(p):
        return rmsnorm(heads(p[6], ATT_H), qk_g[0])

    def kv_of(p):
        return rmsnorm(heads(p[7], ATT_KV), qk_g[1]), heads(p[8], ATT_KV)

    k_l, v_l = kv_of(lat)
    k_c, v_c = kv_of(cx)
    q_l = axial_rope(q_of(lat), ang_r, ang_c)
    k_l = axial_rope(k_l, ang_r, ang_c)
    k_all = jnp.concatenate([k_c, k_l], axis=2)
    v_all = jnp.concatenate([v_c, v_l], axis=2)
    att_l = merge(blocked_attend(group_q(q_l), k_all, v_all).reshape(q_l.shape))
    y_lat = jnp.concatenate([gla_out(o_lat, lat[3]), att_l], axis=-1)
    if not ctx_out:
        return y_lat, None
    q_c = q_of(cx)
    att_c = merge(softmax_attend(group_q(q_c), k_c, v_c).reshape(q_c.shape))
    return y_lat, jnp.concatenate([gla_out(o_ctx, cx[3]), att_c], axis=-1)


def mix_cd(ul, uc, conv_w, conv_b, w_qkv, w_gate, b_gate, ml_g, skip, decay_logit, ret_g, ctx_out):
    lat, cx = split_cols(ul, CD_SIZES), split_cols(uc, CD_SIZES)
    bsz, n_lat = ul.shape[0], ul.shape[1]

    def ml_streams(p):
        xc = jax.nn.silu(short_conv(p[0], conv_w, conv_b))
        xch = heads(xc, ML_H)
        q = jnp.einsum('bhld,hde->bhle', xch, w_qkv[0])
        k = jnp.einsum('bhld,hde->bhle', xch, w_qkv[1])
        v = jnp.einsum('bhld,hde->bhle', heads(p[0], ML_H), w_qkv[2])
        gin = jnp.concatenate([merge(q), merge(k), merge(v)], axis=-1)
        dirs = []
        for d in range(2):
            z = jnp.swapaxes((gin @ w_gate[d] + b_gate[d]).astype(jnp.float32), 1, 2)
            dirs.append((q * ML_HD ** -0.5, k, v, z[:, :ML_H], jax.nn.log_sigmoid(z[:, ML_H:])))
        return xc, dirs[0], dirs[1]

    xc_l, lat_f, lat_b = ml_streams(lat)
    xc_c, ctx_f, ctx_b = ml_streams(cx)
    s0 = (jnp.zeros((bsz, ML_H, ML_HD, ML_HD), jnp.float32), jnp.zeros((bsz, ML_H, ML_HD), jnp.float32),
          jnp.zeros((bsz, ML_H), jnp.float32))
    h_lat, h_ctx = bidir_two_stream(mlstm_dir, lat_f, lat_b, ctx_f, ctx_b, s0, ctx_out)

    def ml_out(h, xc, z):
        return (merge(rmsnorm(h, ml_g)) + skip * xc) * jax.nn.silu(z)

    ang_r, ang_c = grid_angles(n_lat, RET_HD)
    log_decay = jax.nn.log_sigmoid(decay_logit.astype(jnp.float32))

    def ret_streams(p, rotate):
        q, k, v = heads(p[2], RET_H), heads(p[3], RET_H), heads(p[4], RET_H)
        if rotate:
            q, k = axial_rope(q, ang_r, ang_c), axial_rope(k, ang_r, ang_c)
        q = q * RET_HD ** -0.5
        lg = [jnp.broadcast_to(log_decay[d][None, :, None, None], k.shape) for d in range(2)]
        return (q, k, v, lg[0]), (q, k, v, lg[1])

    rl_f, rl_b = ret_streams(lat, True)
    rc_f, rc_b = ret_streams(cx, False)
    s0r = jnp.zeros((bsz, RET_H, RET_HD, RET_HD), jnp.float32)
    r_lat, r_ctx = bidir_two_stream(gla_dir, rl_f, rl_b, rc_f, rc_b, s0r, ctx_out)

    def ret_out(o, g):
        return merge(rmsnorm(o, ret_g)) * jax.nn.silu(g)

    y_lat = jnp.concatenate([ml_out(h_lat, xc_l, lat[1]), ret_out(r_lat, lat[5])], axis=-1)
    if not ctx_out:
        return y_lat, None
    return y_lat, jnp.concatenate([ml_out(h_ctx, xc_c, cx[1]), ret_out(r_ctx, cx[5])], axis=-1)


def setup_inputs(seed: int = 0) -> dict:
    key = jax.random.key(seed)
    ks = iter(jax.random.split(key, 32))

    def nrm(shape, s):
        return jax.random.normal(next(ks), shape, jnp.float32) * s

    f_bias = jnp.linspace(3.0, 6.0, ML_H, dtype=jnp.float32)
    ret_logit = jnp.log(2.0 ** (5.0 + jnp.arange(RET_H, dtype=jnp.float32)) - 1.0)
    return {
        'x': nrm((BATCH, SEQ, D_MODEL), 1.0),
        'c': nrm((BATCH, D_MODEL), 1.0),
        'ctx': nrm((BATCH, CTX_LEN, D_MODEL), 1.0),
        'c_ctx': nrm((D_MODEL,), 1.0),
        'ada_w': nrm((DEPTH, D_MODEL, 6 * D_MODEL), D_MODEL ** -0.5),
        'ada_b': nrm((DEPTH, 6 * D_MODEL), 0.02),
        'norm_g': 1.0 + nrm((DEPTH, 4, D_MODEL), 0.05),
        'w_out': nrm((DEPTH, D_MIX, D_MODEL), D_MIX ** -0.5),
        'mlp_w1': nrm((DEPTH, D_MODEL, D_FF), D_MODEL ** -0.5),
        'mlp_w2': nrm((DEPTH, D_FF, D_MODEL), D_FF ** -0.5),
        'ab_w_in': nrm((N_EVEN, D_MODEL, AB_IN), D_MODEL ** -0.5),
        'gla_w_gate': nrm((N_EVEN, 2, GLA_LR, GLA_H * GLA_DK), GLA_LR ** -0.5),
        'gla_b_gate': nrm((N_EVEN, 2, GLA_H * GLA_DK), 0.02),
        'gla_norm_g': 1.0 + nrm((N_EVEN, GLA_DV), 0.05),
        'att_qk_norm_g': 1.0 + nrm((N_EVEN, 2, ATT_HD), 0.05),
        'cd_w_in': nrm((N_ODD, D_MODEL, CD_IN), D_MODEL ** -0.5),
        'ml_conv_w': nrm((N_ODD, CONV_W, ML_D), CONV_W ** -0.5),
        'ml_conv_b': nrm((N_ODD, ML_D), 0.02),
        'ml_w_qkv': nrm((N_ODD, 3, ML_H, ML_HD, ML_HD), ML_HD ** -0.5),
        'ml_w_gate': nrm((N_ODD, 2, 3 * ML_D, 2 * ML_H), (3 * ML_D) ** -0.5),
        'ml_b_gate': jnp.concatenate([nrm((N_ODD, 2, ML_H), 0.1), f_bias + nrm((N_ODD, 2, ML_H), 0.1)], axis=-1),
        'ml_norm_g': 1.0 + nrm((N_ODD, ML_HD), 0.05),
        'ml_skip': 1.0 + nrm((N_ODD, ML_D), 0.05),
        'ret_decay_logit': ret_logit + nrm((N_ODD, 2, RET_H), 0.1),
        'ret_norm_g': 1.0 + nrm((N_ODD, RET_HD), 0.05),
    }


def reference(x, c, ctx, c_ctx, ada_w, ada_b, norm_g, w_out, mlp_w1, mlp_w2, ab_w_in, gla_w_gate, gla_b_gate,
              gla_norm_g, att_qk_norm_g, cd_w_in, ml_conv_w, ml_conv_b, ml_w_qkv, ml_w_gate, ml_b_gate, ml_norm_g,
              ml_skip, ret_decay_logit, ret_norm_g):
    lat, cx = x, ctx
    for layer in range(DEPTH):
        ctx_out = layer < DEPTH - 1
        g = norm_g[layer]
        mod_l = jnp.split((jax.nn.silu(c) @ ada_w[layer] + ada_b[layer])[:, None, :], 6, axis=-1)
        mod_c = jnp.split(jax.nn.silu(c_ctx) @ ada_w[layer] + ada_b[layer], 6, axis=-1)
        ul = modulate(lat, g[0], mod_l[0], mod_l[1])
        uc = modulate(cx, g[0], mod_c[0], mod_c[1])
        j = layer // 2
        if layer % 2 == 0:
            yl, yc = mix_ab(ul @ ab_w_in[j], uc @ ab_w_in[j], gla_w_gate[j], gla_b_gate[j], gla_norm_g[j],
                            att_qk_norm_g[j], ctx_out)
        else:
            yl, yc = mix_cd(ul @ cd_w_in[j], uc @ cd_w_in[j], ml_conv_w[j], ml_conv_b[j], ml_w_qkv[j],
                            ml_w_gate[j], ml_b_gate[j], ml_norm_g[j], ml_skip[j], ret_decay_logit[j],
                            ret_norm_g[j], ctx_out)
        lat = lat + mod_l[2] * rmsnorm(yl @ w_out[layer], g[1])
        lat = lat + mod_l[5] * rmsnorm(sq_relu_mlp(modulate(lat, g[2], mod_l[3], mod_l[4]), mlp_w1[layer], mlp_w2[layer]), g[3])
        if ctx_out:
            cx = cx + mod_c[2] * rmsnorm(yc @ w_out[layer], g[1])
            cx = cx + mod_c[5] * rmsnorm(sq_relu_mlp(modulate(cx, g[2], mod_c[3], mod_c[4]), mlp_w1[layer], mlp_w2[layer]), g[3])
    return lat
```

```python
import functools

import jax
import jax.numpy as jnp
from jax import lax
from jax.experimental import pallas as pl
from jax.experimental.pallas import tpu as pltpu

F32 = jnp.float32
BF16 = jnp.bfloat16

EPS = 1e-6
ROPE_THETA = 10000.0
GRID_W = 64
CHUNK = 64
LOG2_CHUNK = 6
BLK = 256
NCH = BLK // CHUNK
LANES = 128
GLA_TAU = 16.0
HEAD_DIM = 64
VMEM_LIMIT = 56 * 1024 * 1024


def _cparams(n_axes):
    return pltpu.CompilerParams(dimension_semantics=("arbitrary",) * n_axes,
                                vmem_limit_bytes=VMEM_LIMIT)


def _dot(a, b):
    return jnp.dot(a, b, preferred_element_type=F32)


def _dot_nt(a, b):
    return lax.dot_general(a, b, (((1,), (1,)), ((), ())), preferred_element_type=F32)


def _dot_tn(a, b):
    return lax.dot_general(a, b, (((0,), (0,)), ((), ())), preferred_element_type=F32)


def _split(x, n):
    parts = []
    for _ in range(n - 1):
        p = x.astype(BF16)
        parts.append(p)
        x = x - p.astype(F32)
    parts.append(x.astype(BF16))
    return parts


def _sigmoid(x):
    return 1.0 / (1.0 + jnp.exp(-x))


def _silu(x):
    return x * _sigmoid(x)


def _log_sigmoid(x):
    return jnp.minimum(x, 0.0) - jnp.log(1.0 + jnp.exp(-jnp.abs(x)))


def _rms(x):
    return x * lax.rsqrt(jnp.mean(x * x, axis=-1, keepdims=True) + EPS)


def _head_rms(x, ones_bd):
    hi, lo = _split(x * x, 2)
    ms = (_dot(hi, ones_bd) + _dot(lo, ones_bd)) * (1.0 / HEAD_DIM)
    return x * lax.rsqrt(ms + EPS)


def _rope(x, cos, sin_signed):
    w = x.shape[1]
    reps = w // LANES
    cos_w = jnp.concatenate([cos] * reps, axis=1)
    sin_w = jnp.concatenate([sin_signed] * reps, axis=1)
    lane = lax.broadcasted_iota(jnp.int32, (1, w), 1)
    partner = jnp.where((lane & 31) < 16, pltpu.roll(x, w - 16, 1), pltpu.roll(x, 16, 1))
    return x * cos_w + partner * sin_w


def _chunk_masks(rev):
    ri = lax.broadcasted_iota(jnp.int32, (BLK, BLK), 0)
    ci = lax.broadcasted_iota(jnp.int32, (BLK, BLK), 1)
    same = (ri >> LOG2_CHUNK) == (ci >> LOG2_CHUNK)
    incl = same & ((ci >= ri) if rev else (ci <= ri))
    return same, incl


def _lane_mask(width, group, idx, dtype):
    lane = lax.broadcasted_iota(jnp.int32, (1, width), 1)
    return jnp.where((lane // group) == idx, 1.0, 0.0).astype(dtype)


def _ada_kernel(c_ref, w_ref, b_ref, o_ref):
    sc = _silu(c_ref[...]).astype(BF16)
    o_ref[...] = _dot(sc, w_ref[...].astype(BF16)) + b_ref[...]


def _ada_call(c16, ada_w, ada_b):
    depth, d, n = ada_w.shape
    tn = 1024
    return pl.pallas_call(
        _ada_kernel,
        grid=(depth, n // tn),
        in_specs=[pl.BlockSpec((16, d), lambda l, j: (0, 0)),
                  pl.BlockSpec((None, d, tn), lambda l, j: (l, 0, j)),
                  pl.BlockSpec((None, 1, tn), lambda l, j: (l, 0, j))],
        out_specs=pl.BlockSpec((None, 16, tn), lambda l, j: (l, 0, j)),
        out_shape=jax.ShapeDtypeStruct((depth, 16, n), F32),
        compiler_params=_cparams(2),
        name="ada_mod",
    )(c16, ada_w, ada_b.reshape(depth, 1, n))


def _modulated(x_ref, g_ref, shift_ref, scale_ref):
    y = _rms(x_ref[...]) * g_ref[...]
    return (y * (1.0 + scale_ref[...]) + shift_ref[...]).astype(BF16)


def _inproj_ab_kernel(x_ref, g_ref, shift_ref, scale_ref, w_ref, wg_ref, bg_ref, pq_ref, pk_ref,
                      gq_ref, gk_ref, cos_ref, sin_ref, gla_ref, q_ref, k_ref, v_ref, *, rope):
    u = _modulated(x_ref, g_ref, shift_ref, scale_ref)
    res = _dot(u, w_ref[...])
    gla_ref[:, 0:1536] = res[:, 0:1536]
    z = _dot(res[:, 1536:1664].astype(BF16), wg_ref[...]) + bg_ref[...]
    gla_ref[:, 1536:2048] = _log_sigmoid(z) * (1.0 / GLA_TAU)
    q = _head_rms(res[:, 1664:2176], pq_ref[...]) * gq_ref[...]
    k = _head_rms(res[:, 2176:2432], pk_ref[...]) * gk_ref[...]
    if rope:
        q = _rope(q, cos_ref[...], sin_ref[...])
        k = _rope(k, cos_ref[...], sin_ref[...])
    q_ref[...] = (q * HEAD_DIM ** -0.5).astype(BF16)
    k_ref[...] = k.astype(BF16)
    v_ref[...] = res[:, 2432:2688].astype(BF16)


def _inproj_cd_kernel(x_ref, g_ref, shift_ref, scale_ref, w_ref, cos_ref, sin_ref, ml_ref, ret_ref, *, rope):
    u = _modulated(x_ref, g_ref, shift_ref, scale_ref)
    res = _dot(u, w_ref[...])
    ml_ref[...] = res[:, 0:1024]
    q = res[:, 1024:1536]
    k = res[:, 1536:2048]
    if rope:
        q = _rope(q, cos_ref[...], sin_ref[...])
        k = _rope(k, cos_ref[...], sin_ref[...])
    ret_ref[:, 0:512] = q * HEAD_DIM ** -0.5
    ret_ref[:, 512:1024] = k
    ret_ref[:, 1024:2048] = res[:, 2048:3072]


def _mod_spec(layer, col, row_of):
    return pl.BlockSpec((None, None, 1, 1024), lambda i: (layer, row_of(i), 0, col))


def _const_spec(shape):
    nd = len(shape)
    return pl.BlockSpec(shape, lambda *_: (0,) * nd)


def _inproj_call(kernel, x2d, mod4, g4, layer, row_of, weights, rope_tabs, out_widths, out_dtypes, tm, name):
    rows, d = x2d.shape
    n_pos = rope_tabs[0].shape[0] // tm
    in_specs = [pl.BlockSpec((tm, d), lambda i: (i, 0)),
                pl.BlockSpec((None, None, 1, d), lambda i: (layer, 0, 0, 0)),
                _mod_spec(layer, 0, row_of), _mod_spec(layer, 1, row_of)]
    in_specs += [_const_spec(w.shape) for w in weights]
    in_specs += [pl.BlockSpec((tm, LANES), lambda i: (i % n_pos, 0))] * 2
    return pl.pallas_call(
        kernel,
        grid=(rows // tm,),
        in_specs=in_specs,
        out_specs=[pl.BlockSpec((tm, w), lambda i: (i, 0)) for w in out_widths],
        out_shape=[jax.ShapeDtypeStruct((rows, w), dt) for w, dt in zip(out_widths, out_dtypes)],
        compiler_params=_cparams(1),
        name=name,
    )(x2d, g4, mod4, mod4, *weights, *rope_tabs)


def _lin_block(q, k, vb, lg, st_ref, *, rev, dk):
    hk, hv = q.shape[1], vb.shape[1]
    n_grp = hk // LANES
    hpg = LANES // dk
    vw = hpg * HEAD_DIM
    vpg = LANES // HEAD_DIM
    same, incl = _chunk_masks(rev)
    t_incl = jnp.where(incl, 1.0, 0.0).astype(BF16)
    t_rest = jnp.where(same & jnp.logical_not(incl), 1.0, 0.0).astype(BF16)
    b, r = None, None
    for part in _split(lg, 3):
        pb, pr = _dot(t_incl, part), _dot(t_rest, part)
        b = pb if b is None else b + pb
        r = pr if r is None else r + pr
    qd = (q * jnp.exp(b)).astype(BF16)
    ki = (k * jnp.exp(-b)).astype(BF16)
    kd = (k * jnp.exp(r)).astype(BF16)
    total = b + r

    o_groups = []
    for vg in range(hv // LANES):
        acc = None
        for hh in range(vpg):
            h = vg * vpg + hh
            g, j = (h * dk) // LANES, h % hpg
            sl = slice(g * LANES, (g + 1) * LANES)
            a = _dot_nt(qd[:, sl], ki[:, sl] * _lane_mask(LANES, dk, j, BF16))
            p = jnp.where(incl, a, 0.0).astype(BF16)
            vh = vb[:, vg * LANES:(vg + 1) * LANES] * _lane_mask(LANES, HEAD_DIM, hh, BF16)
            acc = _dot(p, vh) if acc is None else acc + _dot(p, vh)
        o_groups.append(acc)
    o = jnp.concatenate(o_groups, axis=1)

    e_i = lax.broadcasted_iota(jnp.int32, (vw, LANES), 0)
    d_i = lax.broadcasted_iota(jnp.int32, (vw, LANES), 1)
    bd_mask = jnp.where((e_i // HEAD_DIM) == (d_i // dk), 1.0, 0.0)
    rows_out = [None] * NCH
    for c in (range(NCH - 1, -1, -1) if rev else range(NCH)):
        rs = slice(c * CHUNK, (c + 1) * CHUNK)
        decay = jnp.exp(total[c * CHUNK:c * CHUNK + 1, :])
        outs = []
        for g in range(n_grp):
            sl = slice(g * LANES, (g + 1) * LANES)
            st = st_ref[g]
            outs.append(_dot_nt(qd[rs, sl], st.astype(BF16)))
            upd = _dot_tn(vb[rs, g * vw:(g + 1) * vw], kd[rs, sl])
            st_ref[g] = st * decay[:, sl] + upd * bd_mask
        rows_out[c] = jnp.concatenate(outs, axis=1)
    return o + jnp.concatenate(rows_out, axis=0)


def _lin_pass_kernel(*refs, rev, final, ctx_out, dk, hk, q_off, k_off, v_off, g_off, lg_off, q_scale):
    it = iter(refs)
    ctx_ref, lat_ref = next(it), next(it)
    dec_ref = next(it) if lg_off is None else None
    if final:
        of_ctx_ref = next(it) if ctx_out else None
        of_lat_ref, ones_ref, gn_ref = next(it), next(it), next(it)
    o_ctx_ref = next(it) if ctx_out else None
    o_lat_ref, st_ref = next(it), next(it)
    hv = o_lat_ref.shape[1]
    i = pl.program_id(1)

    def run(x_ref, of_ref, o_ref):
        q = x_ref[:, q_off:q_off + hk] * q_scale
        k = x_ref[:, k_off:k_off + hk]
        vb = x_ref[:, v_off:v_off + hv].astype(BF16)
        if lg_off is None:
            lg = jnp.broadcast_to(_log_sigmoid(dec_ref[...]), (BLK, hk))
        else:
            lg = x_ref[:, lg_off:lg_off + hk]
        o = _lin_block(q, k, vb, lg, st_ref, rev=rev, dk=dk)
        if o_ref is None:
            return
        if final:
            o = _head_rms(o + of_ref[...], ones_ref[...]) * gn_ref[...]
            o = o * _silu(x_ref[:, g_off:g_off + hv])
        o_ref[...] = o.astype(o_ref.dtype)

    @pl.when(i == 0)
    def _():
        st_ref[...] = jnp.zeros(st_ref.shape, F32)
        run(ctx_ref, of_ctx_ref if final else None, o_ctx_ref)

    @pl.when(i > 0)
    def _():
        run(lat_ref, of_lat_ref if final else None, o_lat_ref)


def _seq_specs(n_blk, rev, width):
    if rev:
        lat_idx = lambda b, i: (b, n_blk - jnp.maximum(i, 1), 0)
    else:
        lat_idx = lambda b, i: (b, jnp.maximum(i, 1) - 1, 0)
    return (pl.BlockSpec((None, BLK, width), lambda b, i: (b, 0, 0)),
            pl.BlockSpec((None, BLK, width), lat_idx))


def _lin_pass(ctx, lat, dec, of_ctx, of_lat, ones_bd, gn, *, rev, final, ctx_out, dk, hk, hv,
              q_off, k_off, v_off, g_off, lg_off, q_scale, name):
    bsz, n_lat, width = lat.shape
    n_blk = n_lat // BLK
    ctx_spec, lat_spec = _seq_specs(n_blk, rev, width)
    octx_spec, olat_spec = _seq_specs(n_blk, rev, hv)
    args, in_specs = [ctx, lat], [ctx_spec, lat_spec]
    if lg_off is None:
        args.append(dec)
        in_specs.append(_const_spec(dec.shape))
    if final:
        if ctx_out:
            args.append(of_ctx)
            in_specs.append(octx_spec)
        args += [of_lat, ones_bd, gn]
        in_specs += [olat_spec, _const_spec(ones_bd.shape), _const_spec(gn.shape)]
    out_dtype = BF16 if final else F32
    out_specs, out_shape = [], []
    if ctx_out:
        out_specs.append(octx_spec)
        out_shape.append(jax.ShapeDtypeStruct((bsz, BLK, hv), out_dtype))
    out_specs.append(olat_spec)
    out_shape.append(jax.ShapeDtypeStruct((bsz, n_lat, hv), out_dtype))
    n_grp, vw = hk // LANES, (LANES // dk) * HEAD_DIM
    kern = functools.partial(_lin_pass_kernel, rev=rev, final=final, ctx_out=ctx_out, dk=dk, hk=hk,
                             q_off=q_off, k_off=k_off, v_off=v_off, g_off=g_off, lg_off=lg_off,
                             q_scale=q_scale)
    outs = pl.pallas_call(
        kern,
        grid=(bsz, n_blk + 1),
        in_specs=in_specs,
        out_specs=out_specs,
        out_shape=out_shape,
        scratch_shapes=[pltpu.VMEM((n_grp, vw, LANES), F32)],
        compiler_params=_cparams(2),
        name=name,
    )(*args)
    return (outs[0], outs[1]) if ctx_out else (None, outs[0])


def _lin_mixer(ctx, lat, dec2, ones_bd, gn, *, ctx_out, dk, hk, hv, q_off, k_off, v_off, g_off,
               lg_offs, q_scale, name):
    kw = dict(ctx_out=ctx_out, dk=dk, hk=hk, hv=hv, q_off=q_off, k_off=k_off, v_off=v_off,
              g_off=g_off, q_scale=q_scale)
    dec_f, dec_b = (None, None) if dec2 is None else (dec2[0], dec2[1])
    f_ctx, f_lat = _lin_pass(ctx, lat, dec_f, None, None, None, None, rev=False, final=False,
                             lg_off=lg_offs[0], name=name + "_fwd", **kw)
    return _lin_pass(ctx, lat, dec_b, f_ctx, f_lat, ones_bd, gn, rev=True, final=True,
                     lg_off=lg_offs[1], name=name + "_bwd", **kw)


def _attn_kernel(q_ref, kc_ref, vc_ref, kl_ref, vl_ref, o_ref, *, tk, n_lat_blk):
    tq = q_ref.shape[0]
    q = q_ref[...]
    lane = lax.broadcasted_iota(jnp.int32, (1, LANES), 1)
    lo = jnp.where(lane < HEAD_DIM, 1.0, 0.0).astype(BF16)
    qs = jnp.concatenate([q * lo, q * (1.0 - lo)], axis=0)

    def step(carry, kb, vb):
        m, l, acc = carry
        s = _dot_nt(qs, kb)
        m_new = jnp.maximum(m, jnp.max(s, axis=1, keepdims=True))
        alpha = jnp.exp(m - m_new)
        p = jnp.exp(s - m_new)
        l = alpha * l + jnp.sum(p, axis=1, keepdims=True)
        acc = alpha * acc + _dot(p.astype(BF16), vb)
        return m_new, l, acc

    carry = (jnp.full((2 * tq, 1), -jnp.inf, F32), jnp.zeros((2 * tq, 1), F32),
             jnp.zeros((2 * tq, LANES), F32))
    carry = step(carry, kc_ref[...], vc_ref[...])

    def body(j, carry):
        off = pl.multiple_of(j * tk, tk)
        return step(carry, kl_ref[pl.ds(off, tk), :], vl_ref[pl.ds(off, tk), :])

    if n_lat_blk:
        carry = lax.fori_loop(0, n_lat_blk, body, carry)
    _, l, acc = carry
    out = acc / l
    o_ref[...] = jnp.where(lane < HEAD_DIM, out[:tq], out[tq:]).astype(o_ref.dtype)


def _attn_call(q, kc, vc, kl, vl, *, tq, tk, with_lat, name):
    bsz, lq, _ = q.shape
    n_pairs = q.shape[2] // LANES
    kv_of = lambda p: p // 2
    kc_spec = pl.BlockSpec((None, kc.shape[1], LANES), lambda b, p, i: (b, 0, kv_of(p)))
    kl_spec = pl.BlockSpec((None, kl.shape[1], LANES), lambda b, p, i: (b, 0, kv_of(p)))
    kern = functools.partial(_attn_kernel, tk=tk, n_lat_blk=(kl.shape[1] // tk) if with_lat else 0)
    return pl.pallas_call(
        kern,
        grid=(bsz, n_pairs, lq // tq),
        in_specs=[pl.BlockSpec((None, tq, LANES), lambda b, p, i: (b, i, p)),
                  kc_spec, kc_spec, kl_spec, kl_spec],
        out_specs=pl.BlockSpec((None, tq, LANES), lambda b, p, i: (b, i, p)),
        out_shape=jax.ShapeDtypeStruct(q.shape, BF16),
        compiler_params=_cparams(3),
        name=name,
    )(q, kc, vc, kl, vl)


def _ml_prep_kernel(x_ref, prev_ref, next_ref, cw_ref, cb_ref, wq_ref, wk_ref, wv_ref, wg_ref, bg_ref,
                    main_ref, gate_ref):
    i, n = pl.program_id(1), pl.num_programs(1)
    rows, d = x_ref.shape[0], cw_ref.shape[1]
    x = x_ref[:, 0:d]
    row = lax.broadcasted_iota(jnp.int32, (rows, 1), 0)
    before = prev_ref[7:8, 0:d] * jnp.where(i > 0, 1.0, 0.0)
    after = next_ref[0:1, 0:d] * jnp.where(i < n - 1, 1.0, 0.0)
    x_prev = jnp.where(row == 0, before, pltpu.roll(x, 1, 0))
    x_next = jnp.where(row == rows - 1, after, pltpu.roll(x, rows - 1, 0))
    xc = _silu(cw_ref[0:1, :] * x_prev + cw_ref[1:2, :] * x + cw_ref[2:3, :] * x_next + cb_ref[...])
    xcb = xc.astype(BF16)
    q = _dot(xcb, wq_ref[...])
    k = _dot(xcb, wk_ref[...])
    v = _dot(x.astype(BF16), wv_ref[...])
    main_ref[:, 0:d] = xc
    main_ref[:, d:2 * d] = q
    main_ref[:, 2 * d:3 * d] = k
    main_ref[:, 3 * d:4 * d] = v
    gin = jnp.concatenate([q, k, v], axis=1).astype(BF16)
    gate_ref[...] = _dot_nt(wg_ref[...], gin) + bg_ref[...]


def _ml_prep_call(ml, conv_w, conv_b, wq, wk, wv, wg_t, bg_t, name):
    bsz, n_tok, width = ml.shape
    d = conv_w.shape[1]
    n_blk, hb = n_tok // BLK, BLK // 8
    consts = [conv_w, conv_b, wq, wk, wv, wg_t, bg_t]
    return pl.pallas_call(
        _ml_prep_kernel,
        grid=(bsz, n_blk),
        in_specs=[pl.BlockSpec((None, BLK, width), lambda b, i: (b, i, 0)),
                  pl.BlockSpec((None, 8, width), lambda b, i: (b, jnp.maximum(i * hb - 1, 0), 0)),
                  pl.BlockSpec((None, 8, width), lambda b, i: (b, jnp.minimum((i + 1) * hb, n_blk * hb - 1), 0))]
                 + [_const_spec(c.shape) for c in consts],
        out_specs=[pl.BlockSpec((None, BLK, 4 * d), lambda b, i: (b, i, 0)),
                   pl.BlockSpec((None, 32, BLK), lambda b, i: (b, 0, i))],
        out_shape=[jax.ShapeDtypeStruct((bsz, n_tok, 4 * d), F32),
                   jax.ShapeDtypeStruct((bsz, 32, n_tok), F32)],
        compiler_params=_cparams(2),
        name=name,
    )(ml, ml, ml, *consts)


def _ml_block(q, k, v, ig, lf, st_ref, m_ref, expand_ref, *, rev):
    n_heads, d_all = ig.shape[0], q.shape[1]
    n_grp = d_all // LANES
    same, incl = _chunk_masks(rev)
    ri = lax.broadcasted_iota(jnp.int32, (BLK, BLK), 0)
    ci = lax.broadcasted_iota(jnp.int32, (BLK, BLK), 1)
    t_pre = jnp.where(same & ((ri >= ci) if rev else (ri <= ci)), 1.0, 0.0).astype(BF16)
    t_same = jnp.where(same, 1.0, 0.0).astype(BF16)
    zeros8 = jnp.zeros((8, BLK), F32)
    brow, bend = None, None
    for part in _split(jnp.concatenate([lf, zeros8], axis=0), 3):
        pb, pe = _dot(part, t_pre), _dot(part, t_same)
        brow = pb if brow is None else brow + pb
        bend = pe if bend is None else bend + pe
    brow, bend = brow[0:n_heads], bend[0:n_heads]
    a_row = bend - brow + ig
    ncr = ig - brow

    lane = lax.broadcasted_iota(jnp.int32, (1, BLK), 1)
    chunk_of = lane >> LOG2_CHUNK
    pos = lane & (CHUNK - 1)
    neg_inf = jnp.float32(-jnp.inf)

    m_s = m_ref[:, 0:1]
    mst_row = jnp.zeros((n_heads, BLK), F32)
    mloc_row = jnp.zeros((n_heads, BLK), F32)
    s_old, s_new = [None] * NCH, [None] * NCH
    for c in (range(NCH - 1, -1, -1) if rev else range(NCH)):
        in_c = chunk_of == c
        bend_c = jnp.max(jnp.where(in_c, bend, neg_inf), axis=1, keepdims=True)
        mloc_c = jnp.max(jnp.where(in_c, a_row, neg_inf), axis=1, keepdims=True)
        m_new = jnp.maximum(bend_c + m_s, mloc_c)
        s_old[c] = jnp.exp(bend_c + m_s - m_new)
        s_new[c] = jnp.exp(mloc_c - m_new)
        mst_row = jnp.where(in_c, m_s, mst_row)
        mloc_row = jnp.where(in_c, mloc_c, mloc_row)
        m_s = m_new
    m_ref[...] = jnp.broadcast_to(m_s, m_ref.shape)

    cm = ncr
    for sh in (1, 2, 4, 8, 16, 32):
        if rev:
            shifted, valid = pltpu.roll(cm, BLK - sh, 1), pos < CHUNK - sh
        else:
            shifted, valid = pltpu.roll(cm, sh, 1), pos >= sh
        cm = jnp.where(valid, jnp.maximum(cm, shifted), cm)
    g_row = jnp.maximum(mst_row, cm)
    w_row = jnp.exp(a_row - mloc_row)
    sint_row = jnp.exp(mst_row - g_row)
    floor_row = jnp.exp(-(brow + g_row))

    expand2 = expand_ref[...]

    def expand_pair(top, bottom, n_terms):
        acc = None
        for part in _split(jnp.concatenate([top, bottom], axis=0), n_terms):
            term = _dot_tn(part, expand2)
            acc = term if acc is None else acc + term
        return acc[:, 0:d_all], acc[:, d_all:2 * d_all]

    w_exp, sint_exp = expand_pair(w_row, sint_row, 2)
    g_exp, floor_exp = expand_pair(g_row, floor_row, 3)

    lane128 = lax.broadcasted_iota(jnp.int32, (1, LANES), 1)
    ss = jnp.zeros((n_heads, LANES), F32)
    for c in range(NCH):
        ss = jnp.where(lane128 == c, s_old[c], ss)
        ss = jnp.where(lane128 == NCH + c, s_new[c], ss)
    ss = jnp.concatenate([ss, jnp.zeros((8, LANES), F32)], axis=0)
    scal = None
    for part in _split(ss, 3):
        term = _dot_tn(part, expand2[:, 0:d_all])
        scal = term if scal is None else scal + term

    qb, kb, vb = q.astype(BF16), k.astype(BF16), v.astype(BF16)
    kw = (k * w_exp).astype(BF16)
    ones_blk = jnp.ones((BLK, LANES), BF16)

    num_groups, den_groups = [], []
    for g in range(n_grp):
        sl = slice(g * LANES, (g + 1) * LANES)
        nd = None
        for hh in range(LANES // HEAD_DIM):
            h = g * (LANES // HEAD_DIM) + hh
            hm = _lane_mask(LANES, HEAD_DIM, hh, BF16)
            s = _dot_nt(qb[:, sl], kb[:, sl] * hm)
            gcol = g_exp[:, h * HEAD_DIM:h * HEAD_DIM + 1]
            e = jnp.exp(jnp.where(incl, ncr[h:h + 1, :] - gcol, neg_inf))
            p = (s * e).astype(BF16)
            vaug = jnp.concatenate([vb[:, sl] * hm, ones_blk * hm], axis=1)
            term = _dot(p, vaug)
            nd = term if nd is None else nd + term
        num_groups.append(nd[:, 0:LANES])
        den_groups.append(nd[:, LANES:2 * LANES])
    num = jnp.concatenate(num_groups, axis=1)
    den = jnp.concatenate(den_groups, axis=1)

    e_i = lax.broadcasted_iota(jnp.int32, (2 * LANES, LANES), 0)
    d_i = lax.broadcasted_iota(jnp.int32, (2 * LANES, LANES), 1)
    bd_mask = jnp.where(((e_i & (LANES - 1)) // HEAD_DIM) == (d_i // HEAD_DIM), 1.0, 0.0)
    ones_chunk = jnp.ones((CHUNK, LANES), BF16)
    num_rows, den_rows = [None] * NCH, [None] * NCH
    for c in (range(NCH - 1, -1, -1) if rev else range(NCH)):
        rs = slice(c * CHUNK, (c + 1) * CHUNK)
        nums, dens = [], []
        for g in range(n_grp):
            sl = slice(g * LANES, (g + 1) * LANES)
            st = st_ref[g]
            oaug = _dot_nt(qb[rs, sl], st.astype(BF16))
            nums.append(oaug[:, 0:LANES])
            dens.append(oaug[:, LANES:2 * LANES])
            vaug = jnp.concatenate([vb[rs, sl], ones_chunk], axis=1)
            upd = _dot_tn(vaug, kw[rs, sl]) * bd_mask
            st_ref[g] = st * scal[c:c + 1, sl] + upd * scal[NCH + c:NCH + c + 1, sl]
        num_rows[c] = jnp.concatenate(nums, axis=1)
        den_rows[c] = jnp.concatenate(dens, axis=1)
    num = num + sint_exp * jnp.concatenate(num_rows, axis=0)
    den = den + sint_exp * jnp.concatenate(den_rows, axis=0)
    return num / jnp.maximum(jnp.abs(den), floor_exp)


def _ml_pass_kernel(*refs, rev, final):
    it = iter(refs)
    ctx_ref, lat_ref, gctx_ref, glat_ref, expand_ref = next(it), next(it), next(it), next(it), next(it)
    if final:
        hf_ref, z_ref, ones_ref, gn_ref, skip_ref = next(it), next(it), next(it), next(it), next(it)
    o_ref, st_ref, m_ref = next(it), next(it), next(it)
    d = o_ref.shape[1]
    i = pl.program_id(1)
    g0 = 16 if rev else 0

    def run(x_ref, gate_ref, write):
        q = x_ref[:, d:2 * d] * HEAD_DIM ** -0.5
        k = x_ref[:, 2 * d:3 * d]
        v = x_ref[:, 3 * d:4 * d]
        ig = gate_ref[g0:g0 + 8, :]
        lf = _log_sigmoid(gate_ref[g0 + 8:g0 + 16, :])
        h = _ml_block(q, k, v, ig, lf, st_ref, m_ref, expand_ref, rev=rev)
        if not write:
            return
        if final:
            h = _head_rms(h + hf_ref[...], ones_ref[...]) * gn_ref[...]
            h = (h + skip_ref[...] * x_ref[:, 0:d]) * _silu(z_ref[:, d:2 * d])
        o_ref[...] = h.astype(o_ref.dtype)

    @pl.when(i == 0)
    def _():
        st_ref[...] = jnp.zeros(st_ref.shape, F32)
        m_ref[...] = jnp.zeros(m_ref.shape, F32)
        run(ctx_ref, gctx_ref, False)

    @pl.when(i > 0)
    def _():
        run(lat_ref, glat_ref, True)


def _ml_pass(ctx, lat, gctx, glat, expand2, hf, ml_lat, ones_bd, gn, skip, *, rev, final, name):
    bsz, n_lat, width = lat.shape
    d = width // 4
    n_blk = n_lat // BLK
    ctx_spec, lat_spec = _seq_specs(n_blk, rev, width)
    _, o_spec = _seq_specs(n_blk, rev, d)
    lat_blk = o_spec.index_map
    gctx_spec = pl.BlockSpec((None, 32, BLK), lambda b, i: (b, 0, 0))
    glat_spec = pl.BlockSpec((None, 32, BLK), lambda b, i: (b, 0, lat_blk(b, i)[1]))
    args = [ctx, lat, gctx, glat, expand2]
    in_specs = [ctx_spec, lat_spec, gctx_spec, glat_spec, _const_spec(expand2.shape)]
    if final:
        args += [hf, ml_lat, ones_bd, gn, skip]
        in_specs += [o_spec, pl.BlockSpec((None, BLK, ml_lat.shape[2]), lat_blk),
                     _const_spec(ones_bd.shape), _const_spec(gn.shape), _const_spec(skip.shape)]
    return pl.pallas_call(
        functools.partial(_ml_pass_kernel, rev=rev, final=final),
        grid=(bsz, n_blk + 1),
        in_specs=in_specs,
        out_specs=o_spec,
        out_shape=jax.ShapeDtypeStruct((bsz, n_lat, d), BF16 if final else F32),
        scratch_shapes=[pltpu.VMEM((d // LANES, 2 * LANES, LANES), F32), pltpu.VMEM((8, LANES), F32)],
        compiler_params=_cparams(2),
        name=name,
    )(*args)


def _out_mlp_kernel(x_ref, ya_ref, yb_ref, gate1_ref, shift2_ref, scale2_ref, gate2_ref,
                    g1_ref, g2_ref, g3_ref, woa_ref, wob_ref, w1_ref, w2_ref, o_ref):
    d = x_ref.shape[1]
    t = _dot(ya_ref[...], woa_ref[...]) + _dot(yb_ref[...], wob_ref[...])
    lat = x_ref[...] + gate1_ref[...] * (_rms(t) * g1_ref[...])
    u = (_rms(lat) * g2_ref[...] * (1.0 + scale2_ref[...]) + shift2_ref[...]).astype(BF16)
    acc = None
    for c in range(w1_ref.shape[1] // d):
        h = _dot(u, w1_ref[:, c * d:(c + 1) * d])
        h = jnp.square(jnp.maximum(h, 0.0)).astype(BF16)
        term = _dot(h, w2_ref[c * d:(c + 1) * d, :])
        acc = term if acc is None else acc + term
    o_ref[...] = lat + gate2_ref[...] * (_rms(acc) * g3_ref[...])


def _out_mlp_call(x2d, ya, yb, mod4, g4, layer, row_of, woa, wob, w1, w2, tm, name):
    rows, d = x2d.shape
    half = ya.shape[1]
    g_spec = lambda j: pl.BlockSpec((None, None, 1, d), lambda i: (layer, j, 0, 0))
    single = dict(pipeline_mode=pl.Buffered(1))
    w_spec = lambda w: pl.BlockSpec(w.shape, lambda i: (0, 0), **single)
    return pl.pallas_call(
        _out_mlp_kernel,
        grid=(rows // tm,),
        in_specs=[pl.BlockSpec((tm, d), lambda i: (i, 0)),
                  pl.BlockSpec((tm, half), lambda i: (i, 0)),
                  pl.BlockSpec((tm, half), lambda i: (i, 0)),
                  _mod_spec(layer, 2, row_of), _mod_spec(layer, 3, row_of),
                  _mod_spec(layer, 4, row_of), _mod_spec(layer, 5, row_of),
                  g_spec(1), g_spec(2), g_spec(3),
                  w_spec(woa), w_spec(wob), w_spec(w1), w_spec(w2)],
        out_specs=pl.BlockSpec((tm, d), lambda i: (i, 0)),
        out_shape=jax.ShapeDtypeStruct((rows, d), F32),
        compiler_params=_cparams(1),
        name=name,
    )(x2d, ya, yb, mod4, mod4, mod4, mod4, g4, g4, g4, woa, wob, w1, w2)


def _rope_tables(n_tokens):
    n_rows = n_tokens // GRID_W
    rows, cols = jnp.meshgrid(jnp.arange(n_rows), jnp.arange(GRID_W), indexing="ij")
    axis_dim = HEAD_DIM // 2
    inv_freq = ROPE_THETA ** (-jnp.arange(0, axis_dim, 2, dtype=F32) / axis_dim)
    ang_r = rows.reshape(-1, 1).astype(F32) * inv_freq
    ang_c = cols.reshape(-1, 1).astype(F32) * inv_freq
    cos = jnp.concatenate([jnp.cos(ang_r)] * 2 + [jnp.cos(ang_c)] * 2, axis=1)
    sin = jnp.concatenate([-jnp.sin(ang_r), jnp.sin(ang_r), -jnp.sin(ang_c), jnp.sin(ang_c)], axis=1)
    return jnp.tile(cos, (1, 2)), jnp.tile(sin, (1, 2))


def _group_ones(n, group=HEAD_DIM):
    idx = jnp.arange(n) // group
    return (idx[:, None] == idx[None, :]).astype(BF16)


def _block_diag(w):
    n_h, d, e = w.shape
    eye = jnp.eye(n_h, dtype=w.dtype)
    return (eye[:, None, :, None] * w[:, :, None, :]).reshape(n_h * d, n_h * e)


def kernel(x, c, ctx, c_ctx, ada_w, ada_b, norm_g, w_out, mlp_w1, mlp_w2, ab_w_in, gla_w_gate, gla_b_gate,
           gla_norm_g, att_qk_norm_g, cd_w_in, ml_conv_w, ml_conv_b, ml_w_qkv, ml_w_gate, ml_b_gate, ml_norm_g,
           ml_skip, ret_decay_logit, ret_norm_g):
    bsz, n_lat, d = x.shape
    n_ctx = ctx.shape[1]
    depth = ada_w.shape[0]
    assert n_ctx == BLK and n_lat % BLK == 0 and depth == 2 and d == 1024
    tm = 512
    lat_row = lambda i: i // (n_lat // tm)
    ctx_row = lambda i: bsz

    c16 = jnp.concatenate([c, c_ctx[None], jnp.zeros((16 - bsz - 1, d), F32)], axis=0)
    mod4 = _ada_call(c16, ada_w, ada_b).reshape(depth, 16, 1, 6 * d)
    g4 = norm_g.reshape(depth, 4, 1, d)
    cos, sin = _rope_tables(n_lat)
    ones512, ones256 = _group_ones(512), _group_ones(256)

    lat = x.reshape(bsz * n_lat, d)
    cx = ctx.reshape(bsz * n_ctx, d)

    w = ab_w_in[0]
    wk, wv = w[:, 2080:2208], w[:, 2208:2336]
    dup = lambda m: jnp.concatenate([m[:, :64], m[:, :64], m[:, 64:], m[:, 64:]], axis=1)
    w0 = jnp.concatenate([w[:, :1568], jnp.zeros((d, 96), F32), w[:, 1568:2080], dup(wk), dup(wv)],
                         axis=1).astype(BF16)
    wg = jnp.zeros((LANES, 512), F32)
    wg = wg.at[0:16, 0:256].set(gla_w_gate[0, 0]).at[16:32, 256:512].set(gla_w_gate[0, 1]).astype(BF16)
    bg = gla_b_gate[0].reshape(1, 512)
    gq = jnp.tile(att_qk_norm_g[0, 0], 8).reshape(1, 512)
    gk = jnp.tile(att_qk_norm_g[0, 1], 4).reshape(1, 256)
    weights0 = [w0, wg, bg, ones512, ones256, gq, gk]

    def inproj0(x2d, row_of, rope, name):
        return _inproj_call(functools.partial(_inproj_ab_kernel, rope=rope), x2d, mod4, g4, 0, row_of,
                            weights0, (cos, sin), (2048, 512, 256, 256), (F32, BF16, BF16, BF16), tm, name)

    gla_l, q_l, k_l, v_l = inproj0(lat, lat_row, True, "inproj0_lat")
    gla_c, q_c, k_c, v_c = inproj0(cx, ctx_row, False, "inproj0_ctx")
    r3 = lambda a, n: a.reshape(bsz, n, a.shape[-1])
    gn_gla = jnp.tile(gla_norm_g[0], 8).reshape(1, 512)
    ya_c, ya_l = _lin_mixer(r3(gla_c, n_ctx), r3(gla_l, n_lat), None, ones512, gn_gla, ctx_out=True,
                            dk=32, hk=256, hv=512, q_off=0, k_off=256, v_off=512, g_off=1024,
                            lg_offs=(1536, 1792), q_scale=32 ** -0.5, name="gla")
    k_c3, v_c3 = r3(k_c, n_ctx), r3(v_c, n_ctx)
    yb_l = _attn_call(r3(q_l, n_lat), k_c3, v_c3, r3(k_l, n_lat), r3(v_l, n_lat),
                      tq=256, tk=512, with_lat=True, name="attn_lat")
    yb_c = _attn_call(r3(q_c, n_ctx), k_c3, v_c3, k_c3, v_c3, tq=256, tk=256, with_lat=False, name="attn_ctx")

    def out_mlp(x2d, ya, yb, layer, row_of, name):
        wo = w_out[layer].astype(BF16)
        return _out_mlp_call(x2d, ya.reshape(-1, 512), yb.reshape(-1, 512), mod4, g4, layer, row_of,
                             wo[:512], wo[512:], mlp_w1[layer].astype(BF16), mlp_w2[layer].astype(BF16),
                             tm, name)

    lat = out_mlp(lat, ya_l, yb_l, 0, lat_row, "out_mlp0_lat")
    cx = out_mlp(cx, ya_c, yb_c, 0, ctx_row, "out_mlp0_ctx")

    weights1 = [cd_w_in[0].astype(BF16)]

    def inproj1(x2d, row_of, rope, name):
        return _inproj_call(functools.partial(_inproj_cd_kernel, rope=rope), x2d, mod4, g4, 1, row_of,
                            weights1, (cos, sin), (1024, 2048), (F32, F32), tm, name)

    ml_l, ret_l = inproj1(lat, lat_row, True, "inproj1_lat")
    ml_c, ret_c = inproj1(cx, ctx_row, False, "inproj1_ctx")

    wq, wk_m, wv_m = (_block_diag(ml_w_qkv[0, t]).astype(BF16) for t in range(3))
    wg_t = jnp.concatenate([ml_w_gate[0, 0].T, ml_w_gate[0, 1].T], axis=0).astype(BF16)
    bg_t = ml_b_gate[0].reshape(32, 1)
    prep = lambda a, n, name: _ml_prep_call(r3(a, n), ml_conv_w[0], ml_conv_b[0].reshape(1, 512),
                                            wq, wk_m, wv_m, wg_t, bg_t, name)
    main_l, gate_l = prep(ml_l, n_lat, "ml_prep_lat")
    main_c, gate_c = prep(ml_c, n_ctx, "ml_prep_ctx")
    head_of = jnp.arange(512) // HEAD_DIM
    exp8 = (jnp.arange(8)[:, None] == head_of[None, :]).astype(BF16)
    zero8 = jnp.zeros_like(exp8)
    expand2 = jnp.concatenate([jnp.concatenate([exp8, zero8], axis=1),
                               jnp.concatenate([zero8, exp8], axis=1)], axis=0)
    gn_ml = jnp.tile(ml_norm_g[0], 8).reshape(1, 512)
    hf = _ml_pass(main_c, main_l, gate_c, gate_l, expand2, None, None, None, None, None,
                  rev=False, final=False, name="mlstm_fwd")
    ya_l = _ml_pass(main_c, main_l, gate_c, gate_l, expand2, hf, r3(ml_l, n_lat), ones512, gn_ml,
                    ml_skip[0].reshape(1, 512), rev=True, final=True, name="mlstm_bwd")

    dec2 = jnp.repeat(ret_decay_logit[0], HEAD_DIM, axis=1).reshape(2, 1, 512)
    gn_ret = jnp.tile(ret_norm_g[0], 8).reshape(1, 512)
    _, yb_l = _lin_mixer(r3(ret_c, n_ctx), r3(ret_l, n_lat), dec2, ones512, gn_ret, ctx_out=False,
                         dk=64, hk=512, hv=512, q_off=0, k_off=512, v_off=1024, g_off=1536,
                         lg_offs=(None, None), q_scale=1.0, name="ret")

    lat = out_mlp(lat, ya_l, yb_l, 1, lat_row, "out_mlp1_lat")
    return lat.reshape(bsz, n_lat, d)
```

```python
import functools

import jax
import jax.numpy as jnp
from jax import lax
from jax.experimental import pallas as pl
from jax.experimental.pallas import tpu as pltpu

F32 = jnp.float32
BF16 = jnp.bfloat16

EPS = 1e-6
ROPE_THETA = 10000.0
GRID_W = 64
CHUNK = 64
LOG2_CHUNK = 6
BLK = 256
NCH = BLK // CHUNK
LANES = 128
GLA_TAU = 16.0
HEAD_DIM = 64
VMEM_LIMIT = 56 * 1024 * 1024


def _cparams(n_axes):
    return pltpu.CompilerParams(dimension_semantics=("arbitrary",) * n_axes,
                                vmem_limit_bytes=VMEM_LIMIT)


def _dot(a, b):
    return jnp.dot(a, b, preferred_element_type=F32)


def _dot_nt(a, b):
    return lax.dot_general(a, b, (((1,), (1,)), ((), ())), preferred_element_type=F32)


def _dot_tn(a, b):
    return lax.dot_general(a, b, (((0,), (0,)), ((), ())), preferred_element_type=F32)


def _split(x, n):
    parts = []
    for _ in range(n - 1):
        p = x.astype(BF16)
        parts.append(p)
        x = x - p.astype(F32)
    parts.append(x.astype(BF16))
    return parts


def _sigmoid(x):
    return 1.0 / (1.0 + jnp.exp(-x))


def _silu(x):
    return x * _sigmoid(x)


def _log_sigmoid(x):
    return jnp.minimum(x, 0.0) - jnp.log(1.0 + jnp.exp(-jnp.abs(x)))


def _rms(x):
    return x * lax.rsqrt(jnp.mean(x * x, axis=-1, keepdims=True) + EPS)


def _head_rms(x, ones_bd):
    hi, lo = _split(x * x, 2)
    ms = (_dot(hi, ones_bd) + _dot(lo, ones_bd)) * (1.0 / HEAD_DIM)
    return x * lax.rsqrt(ms + EPS)


def _rope(x, cos, sin_signed):
    w = x.shape[1]
    reps = w // LANES
    cos_w = jnp.concatenate([cos] * reps, axis=1)
    sin_w = jnp.concatenate([sin_signed] * reps, axis=1)
    lane = lax.broadcasted_iota(jnp.int32, (1, w), 1)
    partner = jnp.where((lane & 31) < 16, pltpu.roll(x, w - 16, 1), pltpu.roll(x, 16, 1))
    return x * cos_w + partner * sin_w


def _chunk_masks(rev):
    ri = lax.broadcasted_iota(jnp.int32, (BLK, BLK), 0)
    ci = lax.broadcasted_iota(jnp.int32, (BLK, BLK), 1)
    same = (ri >> LOG2_CHUNK) == (ci >> LOG2_CHUNK)
    incl = same & ((ci >= ri) if rev else (ci <= ri))
    return same, incl


def _lane_mask(width, group, idx, dtype):
    lane = lax.broadcasted_iota(jnp.int32, (1, width), 1)
    return jnp.where((lane // group) == idx, 1.0, 0.0).astype(dtype)


def _ada_kernel(c_ref, w_ref, b_ref, o_ref):
    sc = _silu(c_ref[...]).astype(BF16)
    o_ref[...] = _dot(sc, w_ref[...].astype(BF16)) + b_ref[...]


def _ada_call(c16, ada_w, ada_b):
    depth, d, n = ada_w.shape
    tn = 1024
    return pl.pallas_call(
        _ada_kernel,
        grid=(depth, n // tn),
        in_specs=[pl.BlockSpec((16, d), lambda l, j: (0, 0)),
                  pl.BlockSpec((None, d, tn), lambda l, j: (l, 0, j)),
                  pl.BlockSpec((None, 1, tn), lambda l, j: (l, 0, j))],
        out_specs=pl.BlockSpec((None, 16, tn), lambda l, j: (l, 0, j)),
        out_shape=jax.ShapeDtypeStruct((depth, 16, n), F32),
        compiler_params=_cparams(2),
        name="ada_mod",
    )(c16, ada_w, ada_b.reshape(depth, 1, n))


def _modulated(x_ref, g_ref, shift_ref, scale_ref):
    y = _rms(x_ref[...]) * g_ref[...]
    return (y * (1.0 + scale_ref[...]) + shift_ref[...]).astype(BF16)


def _inproj_ab_kernel(x_ref, g_ref, shift_ref, scale_ref, w_ref, wg_ref, bg_ref, pq_ref, pk_ref,
                      gq_ref, gk_ref, cos_ref, sin_ref, gla_ref, q_ref, k_ref, v_ref, *, rope):
    u = _modulated(x_ref, g_ref, shift_ref, scale_ref)
    res = _dot(u, w_ref[...])
    gla_ref[:, 0:1536] = res[:, 0:1536]
    z = _dot(res[:, 1536:1664].astype(BF16), wg_ref[...]) + bg_ref[...]
    gla_ref[:, 1536:2048] = _log_sigmoid(z) * (1.0 / GLA_TAU)
    q = _head_rms(res[:, 1664:2176], pq_ref[...]) * gq_ref[...]
    k = _head_rms(res[:, 2176:2432], pk_ref[...]) * gk_ref[...]
    if rope:
        q = _rope(q, cos_ref[...], sin_ref[...])
        k = _rope(k, cos_ref[...], sin_ref[...])
    q_ref[...] = (q * HEAD_DIM ** -0.5).astype(BF16)
    k_ref[...] = k.astype(BF16)
    v_ref[...] = res[:, 2432:2688].astype(BF16)


def _inproj_cd_kernel(x_ref, g_ref, shift_ref, scale_ref, w_ref, cos_ref, sin_ref, ml_ref, ret_ref, *, rope):
    u = _modulated(x_ref, g_ref, shift_ref, scale_ref)
    res = _dot(u, w_ref[...])
    ml_ref[...] = res[:, 0:1024]
    q = res[:, 1024:1536]
    k = res[:, 1536:2048]
    if rope:
        q = _rope(q, cos_ref[...], sin_ref[...])
        k = _rope(k, cos_ref[...], sin_ref[...])
    ret_ref[:, 0:512] = q * HEAD_DIM ** -0.5
    ret_ref[:, 512:1024] = k
    ret_ref[:, 1024:2048] = res[:, 2048:3072]


def _mod_spec(layer, col, row_of):
    return pl.BlockSpec((None, None, 1, 1024), lambda i: (layer, row_of(i), 0, col))


def _const_spec(shape):
    nd = len(shape)
    return pl.BlockSpec(shape, lambda *_: (0,) * nd)


def _inproj_call(kernel, x2d, mod4, g4, layer, row_of, weights, rope_tabs, out_widths, out_dtypes, tm, name):
    rows, d = x2d.shape
    n_pos = rope_tabs[0].shape[0] // tm
    in_specs = [pl.BlockSpec((tm, d), lambda i: (i, 0)),
                pl.BlockSpec((None, None, 1, d), lambda i: (layer, 0, 0, 0)),
                _mod_spec(layer, 0, row_of), _mod_spec(layer, 1, row_of)]
    in_specs += [_const_spec(w.shape) for w in weights]
    in_specs += [pl.BlockSpec((tm, LANES), lambda i: (i % n_pos, 0))] * 2
    return pl.pallas_call(
        kernel,
        grid=(rows // tm,),
        in_specs=in_specs,
        out_specs=[pl.BlockSpec((tm, w), lambda i: (i, 0)) for w in out_widths],
        out_shape=[jax.ShapeDtypeStruct((rows, w), dt) for w, dt in zip(out_widths, out_dtypes)],
        compiler_params=_cparams(1),
        name=name,
    )(x2d, g4, mod4, mod4, *weights, *rope_tabs)


def _lin_block(q, k, vb, lg, st_ref, *, rev, dk, const_decay):
    hk, hv = q.shape[1], vb.shape[1]
    n_grp = hk // LANES
    hpg = LANES // dk
    vw = hpg * HEAD_DIM
    vpg = LANES // HEAD_DIM
    _, incl = _chunk_masks(rev)
    if const_decay:
        pos = lax.broadcasted_iota(jnp.int32, (BLK, 1), 0) & (CHUNK - 1)
        steps = ((CHUNK - pos) if rev else (pos + 1)).astype(F32)
        b = steps * lg
        r = float(CHUNK) * lg - b
        totals = [float(CHUNK) * lg] * NCH
    else:
        t_incl = jnp.where(incl, 1.0, 0.0).astype(BF16)
        b = None
        for part in _split(lg, 2):
            pb = _dot(t_incl, part)
            b = pb if b is None else b + pb
        last = 0 if rev else CHUNK - 1
        totals = [b[c * CHUNK + last:c * CHUNK + last + 1, :] for c in range(NCH)]
        r = jnp.concatenate([totals[c] - b[c * CHUNK:(c + 1) * CHUNK, :] for c in range(NCH)], axis=0)
    qd = (q * jnp.exp(b)).astype(BF16)
    ki = (k * jnp.exp(-b)).astype(BF16)
    kd = (k * jnp.exp(r)).astype(BF16)

    o_groups = []
    for vg in range(hv // LANES):
        acc = None
        for hh in range(vpg):
            h = vg * vpg + hh
            g, j = (h * dk) // LANES, h % hpg
            sl = slice(g * LANES, (g + 1) * LANES)
            a = _dot_nt(qd[:, sl], ki[:, sl] * _lane_mask(LANES, dk, j, BF16))
            p = jnp.where(incl, a, 0.0).astype(BF16)
            vh = vb[:, vg * LANES:(vg + 1) * LANES] * _lane_mask(LANES, HEAD_DIM, hh, BF16)
            acc = _dot(p, vh) if acc is None else acc + _dot(p, vh)
        o_groups.append(acc)
    o = jnp.concatenate(o_groups, axis=1)

    e_i = lax.broadcasted_iota(jnp.int32, (vw, LANES), 0)
    d_i = lax.broadcasted_iota(jnp.int32, (vw, LANES), 1)
    bd_mask = jnp.where((e_i // HEAD_DIM) == (d_i // dk), 1.0, 0.0)
    rows_out = [None] * NCH
    for c in (range(NCH - 1, -1, -1) if rev else range(NCH)):
        rs = slice(c * CHUNK, (c + 1) * CHUNK)
        decay = jnp.exp(totals[c])
        outs = []
        for g in range(n_grp):
            sl = slice(g * LANES, (g + 1) * LANES)
            st = st_ref[g]
            outs.append(_dot_nt(qd[rs, sl], st.astype(BF16)))
            upd = _dot_tn(vb[rs, g * vw:(g + 1) * vw], kd[rs, sl])
            st_ref[g] = st * decay[:, sl] + upd * bd_mask
        rows_out[c] = jnp.concatenate(outs, axis=1)
    return o + jnp.concatenate(rows_out, axis=0)


def _lin_dual_kernel(*refs, ctx_out, dk, hk, q_off, k_off, v_off, lg_offs, q_scale):
    it = iter(refs)
    ctx_ref, latf_ref, latb_ref = next(it), next(it), next(it)
    dec_ref = next(it) if lg_offs is None else None
    of_ctx_ref, ob_ctx_ref = (next(it), next(it)) if ctx_out else (None, None)
    of_lat_ref, ob_lat_ref, stf_ref, stb_ref = next(it), next(it), next(it), next(it)
    hv = of_lat_ref.shape[1]
    i = pl.program_id(1)

    def run(x_ref, o_ref, st_ref, rev):
        d = 1 if rev else 0
        q = x_ref[:, q_off:q_off + hk] * q_scale
        k = x_ref[:, k_off:k_off + hk]
        vb = x_ref[:, v_off:v_off + hv].astype(BF16)
        if lg_offs is None:
            lg = _log_sigmoid(dec_ref[d:d + 1, :])
        else:
            lg = x_ref[:, lg_offs[d]:lg_offs[d] + hk]
        o = _lin_block(q, k, vb, lg, st_ref, rev=rev, dk=dk, const_decay=lg_offs is None)
        if o_ref is not None:
            o_ref[...] = o

    @pl.when(i == 0)
    def _():
        stf_ref[...] = jnp.zeros(stf_ref.shape, F32)
        stb_ref[...] = jnp.zeros(stb_ref.shape, F32)
        run(ctx_ref, of_ctx_ref, stf_ref, False)
        run(ctx_ref, ob_ctx_ref, stb_ref, True)

    @pl.when(i > 0)
    def _():
        run(latf_ref, of_lat_ref, stf_ref, False)
        run(latb_ref, ob_lat_ref, stb_ref, True)


def _fwd_blk(i):
    return jnp.maximum(i, 1) - 1


def _bwd_blk(i, n_blk):
    return n_blk - jnp.maximum(i, 1)


def _seq_specs(n_blk, width):
    return (pl.BlockSpec((None, BLK, width), lambda b, i: (b, 0, 0)),
            pl.BlockSpec((None, BLK, width), lambda b, i: (b, _fwd_blk(i), 0)),
            pl.BlockSpec((None, BLK, width), lambda b, i: (b, _bwd_blk(i, n_blk), 0)))


def _lin_mixer(ctx, lat, dec2, *, ctx_out, dk, hk, hv, q_off, k_off, v_off, lg_offs, q_scale, name):
    bsz, n_lat, width = lat.shape
    n_blk = n_lat // BLK
    ctx_spec, latf_spec, latb_spec = _seq_specs(n_blk, width)
    octx_spec, olatf_spec, olatb_spec = _seq_specs(n_blk, hv)
    args, in_specs = [ctx, lat, lat], [ctx_spec, latf_spec, latb_spec]
    if lg_offs is None:
        args.append(dec2)
        in_specs.append(_const_spec(dec2.shape))
    out_specs, out_shape = [], []
    if ctx_out:
        out_specs += [octx_spec, octx_spec]
        out_shape += [jax.ShapeDtypeStruct((bsz, BLK, hv), F32)] * 2
    out_specs += [olatf_spec, olatb_spec]
    out_shape += [jax.ShapeDtypeStruct((bsz, n_lat, hv), F32)] * 2
    n_grp, vw = hk // LANES, (LANES // dk) * HEAD_DIM
    kern = functools.partial(_lin_dual_kernel, ctx_out=ctx_out, dk=dk, hk=hk, q_off=q_off, k_off=k_off,
                             v_off=v_off, lg_offs=lg_offs, q_scale=q_scale)
    outs = pl.pallas_call(
        kern,
        grid=(bsz, n_blk + 1),
        in_specs=in_specs,
        out_specs=out_specs,
        out_shape=out_shape,
        scratch_shapes=[pltpu.VMEM((n_grp, vw, LANES), F32)] * 2,
        compiler_params=_cparams(2),
        name=name,
    )(*args)
    return tuple(outs) if ctx_out else (None, None) + tuple(outs)


def _attn_kernel(q_ref, kc_ref, vc_ref, kl_ref, vl_ref, o_ref, *, tk, n_lat_blk):
    tq = q_ref.shape[0]
    q = q_ref[...]
    lane = lax.broadcasted_iota(jnp.int32, (1, LANES), 1)
    lo = jnp.where(lane < HEAD_DIM, 1.0, 0.0).astype(BF16)
    qs = jnp.concatenate([q * lo, q * (1.0 - lo)], axis=0)

    def step(carry, kb, vb):
        m, l, acc = carry
        s = _dot_nt(qs, kb)
        m_new = jnp.maximum(m, jnp.max(s, axis=1, keepdims=True))
        alpha = jnp.exp(m - m_new)
        p = jnp.exp(s - m_new)
        l = alpha * l + jnp.sum(p, axis=1, keepdims=True)
        acc = alpha * acc + _dot(p.astype(BF16), vb)
        return m_new, l, acc

    carry = (jnp.full((2 * tq, 1), -jnp.inf, F32), jnp.zeros((2 * tq, 1), F32),
             jnp.zeros((2 * tq, LANES), F32))
    carry = step(carry, kc_ref[...], vc_ref[...])

    for j in range(n_lat_blk):
        carry = step(carry, kl_ref[j * tk:(j + 1) * tk, :], vl_ref[j * tk:(j + 1) * tk, :])
    _, l, acc = carry
    out = acc / l
    o_ref[...] = jnp.where(lane < HEAD_DIM, out[:tq], out[tq:]).astype(o_ref.dtype)


def _attn_call(q, kc, vc, kl, vl, *, tq, tk, with_lat, name):
    bsz, lq, _ = q.shape
    n_pairs = q.shape[2] // LANES
    kv_of = lambda p: p // 2
    kc_spec = pl.BlockSpec((None, kc.shape[1], LANES), lambda b, p, i: (b, 0, kv_of(p)))
    kl_spec = pl.BlockSpec((None, kl.shape[1], LANES), lambda b, p, i: (b, 0, kv_of(p)))
    kern = functools.partial(_attn_kernel, tk=tk, n_lat_blk=(kl.shape[1] // tk) if with_lat else 0)
    return pl.pallas_call(
        kern,
        grid=(bsz, n_pairs, lq // tq),
        in_specs=[pl.BlockSpec((None, tq, LANES), lambda b, p, i: (b, i, p)),
                  kc_spec, kc_spec, kl_spec, kl_spec],
        out_specs=pl.BlockSpec((None, tq, LANES), lambda b, p, i: (b, i, p)),
        out_shape=jax.ShapeDtypeStruct(q.shape, BF16),
        compiler_params=_cparams(3),
        name=name,
    )(q, kc, vc, kl, vl)


def _ml_prep_kernel(x_ref, prev_ref, next_ref, cw_ref, cb_ref, wq_ref, wk_ref, wv_ref, wg_ref, bg_ref,
                    main_ref, gate_ref):
    i, n = pl.program_id(1), pl.num_programs(1)
    rows, d = x_ref.shape[0], cw_ref.shape[1]
    x = x_ref[:, 0:d]
    row = lax.broadcasted_iota(jnp.int32, (rows, 1), 0)
    before = prev_ref[7:8, 0:d] * jnp.where(i > 0, 1.0, 0.0)
    after = next_ref[0:1, 0:d] * jnp.where(i < n - 1, 1.0, 0.0)
    x_prev = jnp.where(row == 0, before, pltpu.roll(x, 1, 0))
    x_next = jnp.where(row == rows - 1, after, pltpu.roll(x, rows - 1, 0))
    xc = _silu(cw_ref[0:1, :] * x_prev + cw_ref[1:2, :] * x + cw_ref[2:3, :] * x_next + cb_ref[...])
    xcb = xc.astype(BF16)
    q = _dot(xcb, wq_ref[...])
    k = _dot(xcb, wk_ref[...])
    v = _dot(x.astype(BF16), wv_ref[...])
    main_ref[:, 0:d] = xc
    main_ref[:, d:2 * d] = q
    main_ref[:, 2 * d:3 * d] = k
    main_ref[:, 3 * d:4 * d] = v
    gin = jnp.concatenate([q, k, v], axis=1).astype(BF16)
    gate_ref[...] = _dot_nt(wg_ref[...], gin) + bg_ref[...]


def _ml_prep_call(ml, conv_w, conv_b, wq, wk, wv, wg_t, bg_t, name):
    bsz, n_tok, width = ml.shape
    d = conv_w.shape[1]
    n_blk, hb = n_tok // BLK, BLK // 8
    consts = [conv_w, conv_b, wq, wk, wv, wg_t, bg_t]
    return pl.pallas_call(
        _ml_prep_kernel,
        grid=(bsz, n_blk),
        in_specs=[pl.BlockSpec((None, BLK, width), lambda b, i: (b, i, 0)),
                  pl.BlockSpec((None, 8, width), lambda b, i: (b, jnp.maximum(i * hb - 1, 0), 0)),
                  pl.BlockSpec((None, 8, width), lambda b, i: (b, jnp.minimum((i + 1) * hb, n_blk * hb - 1), 0))]
                 + [_const_spec(c.shape) for c in consts],
        out_specs=[pl.BlockSpec((None, BLK, 4 * d), lambda b, i: (b, i, 0)),
                   pl.BlockSpec((None, 32, BLK), lambda b, i: (b, 0, i))],
        out_shape=[jax.ShapeDtypeStruct((bsz, n_tok, 4 * d), F32),
                   jax.ShapeDtypeStruct((bsz, 32, n_tok), F32)],
        compiler_params=_cparams(2),
        name=name,
    )(ml, ml, ml, *consts)


def _ml_block(q, k, v, ig, lf, st_ref, m_ref, expand_ref, *, rev):
    n_heads, d_all = ig.shape[0], q.shape[1]
    n_grp = d_all // LANES
    same, incl = _chunk_masks(rev)
    ri = lax.broadcasted_iota(jnp.int32, (BLK, BLK), 0)
    ci = lax.broadcasted_iota(jnp.int32, (BLK, BLK), 1)
    t_pre = jnp.where(same & ((ri >= ci) if rev else (ri <= ci)), 1.0, 0.0).astype(BF16)
    t_same = jnp.where(same, 1.0, 0.0).astype(BF16)
    zeros8 = jnp.zeros((8, BLK), F32)
    brow, bend = None, None
    for part in _split(jnp.concatenate([lf, zeros8], axis=0), 3):
        pb, pe = _dot(part, t_pre), _dot(part, t_same)
        brow = pb if brow is None else brow + pb
        bend = pe if bend is None else bend + pe
    brow, bend = brow[0:n_heads], bend[0:n_heads]
    a_row = bend - brow + ig
    ncr = ig - brow

    lane = lax.broadcasted_iota(jnp.int32, (1, BLK), 1)
    chunk_of = lane >> LOG2_CHUNK
    pos = lane & (CHUNK - 1)
    neg_inf = jnp.float32(-jnp.inf)

    m_s = m_ref[:, 0:1]
    mst_row = jnp.zeros((n_heads, BLK), F32)
    mloc_row = jnp.zeros((n_heads, BLK), F32)
    s_old, s_new = [None] * NCH, [None] * NCH
    for c in (range(NCH - 1, -1, -1) if rev else range(NCH)):
        in_c = chunk_of == c
        bend_c = jnp.max(jnp.where(in_c, bend, neg_inf), axis=1, keepdims=True)
        mloc_c = jnp.max(jnp.where(in_c, a_row, neg_inf), axis=1, keepdims=True)
        m_new = jnp.maximum(bend_c + m_s, mloc_c)
        s_old[c] = jnp.exp(bend_c + m_s - m_new)
        s_new[c] = jnp.exp(mloc_c - m_new)
        mst_row = jnp.where(in_c, m_s, mst_row)
        mloc_row = jnp.where(in_c, mloc_c, mloc_row)
        m_s = m_new
    m_ref[...] = jnp.broadcast_to(m_s, m_ref.shape)

    cm = ncr
    for sh in (1, 2, 4, 8, 16, 32):
        if rev:
            shifted, valid = pltpu.roll(cm, BLK - sh, 1), pos < CHUNK - sh
        else:
            shifted, valid = pltpu.roll(cm, sh, 1), pos >= sh
        cm = jnp.where(valid, jnp.maximum(cm, shifted), cm)
    g_row = jnp.maximum(mst_row, cm)
    w_row = jnp.exp(a_row - mloc_row)
    sint_row = jnp.exp(mst_row - g_row)
    floor_row = jnp.exp(-(brow + g_row))

    expand2 = expand_ref[...]

    def expand_pair(top, bottom, n_terms):
        acc = None
        for part in _split(jnp.concatenate([top, bottom], axis=0), n_terms):
            term = _dot_tn(part, expand2)
            acc = term if acc is None else acc + term
        return acc[:, 0:d_all], acc[:, d_all:2 * d_all]

    w_exp, sint_exp = expand_pair(w_row, sint_row, 2)
    g_exp, floor_exp = expand_pair(g_row, floor_row, 3)

    lane128 = lax.broadcasted_iota(jnp.int32, (1, LANES), 1)
    ss = jnp.zeros((n_heads, LANES), F32)
    for c in range(NCH):
        ss = jnp.where(lane128 == c, s_old[c], ss)
        ss = jnp.where(lane128 == NCH + c, s_new[c], ss)
    ss = jnp.concatenate([ss, jnp.zeros((8, LANES), F32)], axis=0)
    scal = None
    for part in _split(ss, 3):
        term = _dot_tn(part, expand2[:, 0:d_all])
        scal = term if scal is None else scal + term

    qb, kb, vb = q.astype(BF16), k.astype(BF16), v.astype(BF16)
    kw = (k * w_exp).astype(BF16)
    ones_blk = jnp.ones((BLK, LANES), BF16)

    num_groups, den_groups = [], []
    for g in range(n_grp):
        sl = slice(g * LANES, (g + 1) * LANES)
        nd = None
        for hh in range(LANES // HEAD_DIM):
            h = g * (LANES // HEAD_DIM) + hh
            hm = _lane_mask(LANES, HEAD_DIM, hh, BF16)
            s = _dot_nt(qb[:, sl], kb[:, sl] * hm)
            gcol = g_exp[:, h * HEAD_DIM:h * HEAD_DIM + 1]
            e = jnp.exp(jnp.where(incl, ncr[h:h + 1, :] - gcol, neg_inf))
            p = (s * e).astype(BF16)
            vaug = jnp.concatenate([vb[:, sl] * hm, ones_blk * hm], axis=1)
            term = _dot(p, vaug)
            nd = term if nd is None else nd + term
        num_groups.append(nd[:, 0:LANES])
        den_groups.append(nd[:, LANES:2 * LANES])
    num = jnp.concatenate(num_groups, axis=1)
    den = jnp.concatenate(den_groups, axis=1)

    e_i = lax.broadcasted_iota(jnp.int32, (2 * LANES, LANES), 0)
    d_i = lax.broadcasted_iota(jnp.int32, (2 * LANES, LANES), 1)
    bd_mask = jnp.where(((e_i & (LANES - 1)) // HEAD_DIM) == (d_i // HEAD_DIM), 1.0, 0.0)
    ones_chunk = jnp.ones((CHUNK, LANES), BF16)
    num_rows, den_rows = [None] * NCH, [None] * NCH
    for c in (range(NCH - 1, -1, -1) if rev else range(NCH)):
        rs = slice(c * CHUNK, (c + 1) * CHUNK)
        nums, dens = [], []
        for g in range(n_grp):
            sl = slice(g * LANES, (g + 1) * LANES)
            st = st_ref[g]
            oaug = _dot_nt(qb[rs, sl], st.astype(BF16))
            nums.append(oaug[:, 0:LANES])
            dens.append(oaug[:, LANES:2 * LANES])
            vaug = jnp.concatenate([vb[rs, sl], ones_chunk], axis=1)
            upd = _dot_tn(vaug, kw[rs, sl]) * bd_mask
            st_ref[g] = st * scal[c:c + 1, sl] + upd * scal[NCH + c:NCH + c + 1, sl]
        num_rows[c] = jnp.concatenate(nums, axis=1)
        den_rows[c] = jnp.concatenate(dens, axis=1)
    num = num + sint_exp * jnp.concatenate(num_rows, axis=0)
    den = den + sint_exp * jnp.concatenate(den_rows, axis=0)
    return num / jnp.maximum(jnp.abs(den), floor_exp)


def _ml_dual_kernel(ctx_ref, latf_ref, latb_ref, gctx_ref, glatf_ref, glatb_ref, expand_ref,
                    hf_ref, hb_ref, stf_ref, mf_ref, stb_ref, mb_ref):
    d = hf_ref.shape[1]
    i = pl.program_id(1)

    def run(x_ref, gate_ref, o_ref, st_ref, m_ref, rev):
        g0 = 16 if rev else 0
        q = x_ref[:, d:2 * d] * HEAD_DIM ** -0.5
        k = x_ref[:, 2 * d:3 * d]
        v = x_ref[:, 3 * d:4 * d]
        ig = gate_ref[g0:g0 + 8, :]
        lf = _log_sigmoid(gate_ref[g0 + 8:g0 + 16, :])
        h = _ml_block(q, k, v, ig, lf, st_ref, m_ref, expand_ref, rev=rev)
        if o_ref is not None:
            o_ref[...] = h

    @pl.when(i == 0)
    def _():
        for ref in (stf_ref, mf_ref, stb_ref, mb_ref):
            ref[...] = jnp.zeros(ref.shape, F32)
        run(ctx_ref, gctx_ref, None, stf_ref, mf_ref, False)
        run(ctx_ref, gctx_ref, None, stb_ref, mb_ref, True)

    @pl.when(i > 0)
    def _():
        run(latf_ref, glatf_ref, hf_ref, stf_ref, mf_ref, False)
        run(latb_ref, glatb_ref, hb_ref, stb_ref, mb_ref, True)


def _ml_mixer(ctx, lat, gctx, glat, expand2, name):
    bsz, n_lat, width = lat.shape
    d = width // 4
    n_blk = n_lat // BLK
    ctx_spec, latf_spec, latb_spec = _seq_specs(n_blk, width)
    _, of_spec, ob_spec = _seq_specs(n_blk, d)
    gate_specs = [pl.BlockSpec((None, 32, BLK), lambda b, i: (b, 0, 0)),
                  pl.BlockSpec((None, 32, BLK), lambda b, i: (b, 0, _fwd_blk(i))),
                  pl.BlockSpec((None, 32, BLK), lambda b, i: (b, 0, _bwd_blk(i, n_blk)))]
    state = [pltpu.VMEM((d // LANES, 2 * LANES, LANES), F32), pltpu.VMEM((8, LANES), F32)]
    return pl.pallas_call(
        _ml_dual_kernel,
        grid=(bsz, n_blk + 1),
        in_specs=[ctx_spec, latf_spec, latb_spec] + gate_specs + [_const_spec(expand2.shape)],
        out_specs=[of_spec, ob_spec],
        out_shape=[jax.ShapeDtypeStruct((bsz, n_lat, d), F32)] * 2,
        scratch_shapes=state + state,
        compiler_params=_cparams(2),
        name=name,
    )(ctx, lat, lat, gctx, glat, glat, expand2)


def _mixer_half(it, mode, ones_bd):
    if mode == "raw":
        return next(it)[...]
    fwd_ref, bwd_ref, gate_ref, gn_ref = next(it), next(it), next(it), next(it)
    o = _head_rms(fwd_ref[...] + bwd_ref[...], ones_bd) * gn_ref[...]
    if mode == "ml":
        skip_ref, xc_ref = next(it), next(it)
        o = o + skip_ref[...] * xc_ref[...]
    return (o * _silu(gate_ref[...])).astype(BF16)


def _out_mlp_kernel(*refs, modes):
    it = iter(refs)
    x_ref, ones_ref = next(it), next(it)
    ya = _mixer_half(it, modes[0], ones_ref[...])
    yb = _mixer_half(it, modes[1], ones_ref[...])
    (gate1_ref, shift2_ref, scale2_ref, gate2_ref, g1_ref, g2_ref, g3_ref,
     woa_ref, wob_ref, w1_ref, w2_ref, o_ref) = it
    d = x_ref.shape[1]
    t = _dot(ya, woa_ref[...]) + _dot(yb, wob_ref[...])
    lat = x_ref[...] + gate1_ref[...] * (_rms(t) * g1_ref[...])
    u = (_rms(lat) * g2_ref[...] * (1.0 + scale2_ref[...]) + shift2_ref[...]).astype(BF16)
    acc = None
    for c in range(w1_ref.shape[1] // d):
        h = _dot(u, w1_ref[:, c * d:(c + 1) * d])
        h = jnp.square(jnp.maximum(h, 0.0)).astype(BF16)
        term = _dot(h, w2_ref[c * d:(c + 1) * d, :])
        acc = term if acc is None else acc + term
    o_ref[...] = lat + gate2_ref[...] * (_rms(acc) * g3_ref[...])


def _out_mlp_call(x2d, halves, ones_bd, mod4, g4, layer, row_of, woa, wob, w1, w2, tm, name):
    rows, d = x2d.shape
    half = woa.shape[0]
    g_spec = lambda j: pl.BlockSpec((None, None, 1, d), lambda i: (layer, j, 0, 0))
    single = dict(pipeline_mode=pl.Buffered(1))
    w_spec = lambda w: pl.BlockSpec(w.shape, lambda i: (0, 0), **single)
    args, in_specs = [x2d, ones_bd], [pl.BlockSpec((tm, d), lambda i: (i, 0)), _const_spec(ones_bd.shape)]
    for _, operands in halves:
        for op in operands:
            if isinstance(op, tuple):
                arr, blk = op
                assert arr.shape[0] == rows
                args.append(arr)
                in_specs.append(pl.BlockSpec((tm, half), lambda i, blk=blk: (i, blk)))
            else:
                args.append(op)
                in_specs.append(_const_spec(op.shape))
    args += [mod4, mod4, mod4, mod4, g4, g4, g4, woa, wob, w1, w2]
    in_specs += [_mod_spec(layer, 2, row_of), _mod_spec(layer, 3, row_of),
                 _mod_spec(layer, 4, row_of), _mod_spec(layer, 5, row_of),
                 g_spec(1), g_spec(2), g_spec(3),
                 w_spec(woa), w_spec(wob), w_spec(w1), w_spec(w2)]
    return pl.pallas_call(
        functools.partial(_out_mlp_kernel, modes=tuple(m for m, _ in halves)),
        grid=(rows // tm,),
        in_specs=in_specs,
        out_specs=pl.BlockSpec((tm, d), lambda i: (i, 0)),
        out_shape=jax.ShapeDtypeStruct((rows, d), F32),
        compiler_params=_cparams(1),
        name=name,
    )(*args)


def _rope_tables(n_tokens):
    n_rows = n_tokens // GRID_W
    rows, cols = jnp.meshgrid(jnp.arange(n_rows), jnp.arange(GRID_W), indexing="ij")
    axis_dim = HEAD_DIM // 2
    inv_freq = ROPE_THETA ** (-jnp.arange(0, axis_dim, 2, dtype=F32) / axis_dim)
    ang_r = rows.reshape(-1, 1).astype(F32) * inv_freq
    ang_c = cols.reshape(-1, 1).astype(F32) * inv_freq
    cos = jnp.concatenate([jnp.cos(ang_r)] * 2 + [jnp.cos(ang_c)] * 2, axis=1)
    sin = jnp.concatenate([-jnp.sin(ang_r), jnp.sin(ang_r), -jnp.sin(ang_c), jnp.sin(ang_c)], axis=1)
    return jnp.tile(cos, (1, 2)), jnp.tile(sin, (1, 2))


def _group_ones(n, group=HEAD_DIM):
    idx = jnp.arange(n) // group
    return (idx[:, None] == idx[None, :]).astype(BF16)


def _block_diag(w):
    n_h, d, e = w.shape
    eye = jnp.eye(n_h, dtype=w.dtype)
    return (eye[:, None, :, None] * w[:, :, None, :]).reshape(n_h * d, n_h * e)


def kernel(x, c, ctx, c_ctx, ada_w, ada_b, norm_g, w_out, mlp_w1, mlp_w2, ab_w_in, gla_w_gate, gla_b_gate,
           gla_norm_g, att_qk_norm_g, cd_w_in, ml_conv_w, ml_conv_b, ml_w_qkv, ml_w_gate, ml_b_gate, ml_norm_g,
           ml_skip, ret_decay_logit, ret_norm_g):
    bsz, n_lat, d = x.shape
    n_ctx = ctx.shape[1]
    depth = ada_w.shape[0]
    assert n_ctx == BLK and n_lat % BLK == 0 and depth == 2 and d == 1024
    tm = 512
    assert n_lat % tm == 0 and (bsz * n_ctx) % tm == 0
    lat_row = lambda i: i // (n_lat // tm)
    ctx_row = lambda i: bsz

    c16 = jnp.concatenate([c, c_ctx[None], jnp.zeros((16 - bsz - 1, d), F32)], axis=0)
    mod4 = _ada_call(c16, ada_w, ada_b).reshape(depth, 16, 1, 6 * d)
    g4 = norm_g.reshape(depth, 4, 1, d)
    cos, sin = _rope_tables(n_lat)
    ones512, ones256 = _group_ones(512), _group_ones(256)

    lat = x.reshape(bsz * n_lat, d)
    cx = ctx.reshape(bsz * n_ctx, d)

    w = ab_w_in[0]
    wk, wv = w[:, 2080:2208], w[:, 2208:2336]
    dup = lambda m: jnp.concatenate([m[:, :64], m[:, :64], m[:, 64:], m[:, 64:]], axis=1)
    w0 = jnp.concatenate([w[:, :1568], jnp.zeros((d, 96), F32), w[:, 1568:2080], dup(wk), dup(wv)],
                         axis=1).astype(BF16)
    wg = jnp.zeros((LANES, 512), F32)
    wg = wg.at[0:16, 0:256].set(gla_w_gate[0, 0]).at[16:32, 256:512].set(gla_w_gate[0, 1]).astype(BF16)
    bg = gla_b_gate[0].reshape(1, 512)
    gq = jnp.tile(att_qk_norm_g[0, 0], 8).reshape(1, 512)
    gk = jnp.tile(att_qk_norm_g[0, 1], 4).reshape(1, 256)
    weights0 = [w0, wg, bg, ones512, ones256, gq, gk]

    def inproj0(x2d, row_of, rope, name):
        return _inproj_call(functools.partial(_inproj_ab_kernel, rope=rope), x2d, mod4, g4, 0, row_of,
                            weights0, (cos, sin), (2048, 512, 256, 256), (F32, BF16, BF16, BF16), tm, name)

    gla_l, q_l, k_l, v_l = inproj0(lat, lat_row, True, "inproj0_lat")
    gla_c, q_c, k_c, v_c = inproj0(cx, ctx_row, False, "inproj0_ctx")
    r3 = lambda a, n: a.reshape(bsz, n, a.shape[-1])
    gn_gla = jnp.tile(gla_norm_g[0], 8).reshape(1, 512)
    gf_c, gb_c, gf_l, gb_l = _lin_mixer(r3(gla_c, n_ctx), r3(gla_l, n_lat), None, ctx_out=True,
                                        dk=32, hk=256, hv=512, q_off=0, k_off=256, v_off=512,
                                        lg_offs=(1536, 1792), q_scale=32 ** -0.5, name="gla")
    k_c3, v_c3 = r3(k_c, n_ctx), r3(v_c, n_ctx)
    yb_l = _attn_call(r3(q_l, n_lat), k_c3, v_c3, r3(k_l, n_lat), r3(v_l, n_lat),
                      tq=256, tk=512, with_lat=True, name="attn_lat")
    yb_c = _attn_call(r3(q_c, n_ctx), k_c3, v_c3, k_c3, v_c3, tq=256, tk=256, with_lat=False, name="attn_ctx")

    r2 = lambda a: a.reshape(-1, a.shape[-1])

    def out_mlp(x2d, halves, layer, row_of, name):
        wo = w_out[layer].astype(BF16)
        return _out_mlp_call(x2d, halves, ones512, mod4, g4, layer, row_of, wo[:512], wo[512:],
                             mlp_w1[layer].astype(BF16), mlp_w2[layer].astype(BF16), tm, name)

    lat = out_mlp(lat, [("lin", [(r2(gf_l), 0), (r2(gb_l), 0), (gla_l, 2), gn_gla]), ("raw", [(r2(yb_l), 0)])],
                  0, lat_row, "out_mlp0_lat")
    cx = out_mlp(cx, [("lin", [(r2(gf_c), 0), (r2(gb_c), 0), (gla_c, 2), gn_gla]), ("raw", [(r2(yb_c), 0)])],
                 0, ctx_row, "out_mlp0_ctx")

    weights1 = [cd_w_in[0].astype(BF16)]

    def inproj1(x2d, row_of, rope, name):
        return _inproj_call(functools.partial(_inproj_cd_kernel, rope=rope), x2d, mod4, g4, 1, row_of,
                            weights1, (cos, sin), (1024, 2048), (F32, F32), tm, name)

    ml_l, ret_l = inproj1(lat, lat_row, True, "inproj1_lat")
    ml_c, ret_c = inproj1(cx, ctx_row, False, "inproj1_ctx")

    wq, wk_m, wv_m = (_block_diag(ml_w_qkv[0, t]).astype(BF16) for t in range(3))
    wg_t = jnp.concatenate([ml_w_gate[0, 0].T, ml_w_gate[0, 1].T], axis=0).astype(BF16)
    bg_t = ml_b_gate[0].reshape(32, 1)
    prep = lambda a, n, name: _ml_prep_call(r3(a, n), ml_conv_w[0], ml_conv_b[0].reshape(1, 512),
                                            wq, wk_m, wv_m, wg_t, bg_t, name)
    main_l, gate_l = prep(ml_l, n_lat, "ml_prep_lat")
    main_c, gate_c = prep(ml_c, n_ctx, "ml_prep_ctx")
    head_of = jnp.arange(512) // HEAD_DIM
    exp8 = (jnp.arange(8)[:, None] == head_of[None, :]).astype(BF16)
    zero8 = jnp.zeros_like(exp8)
    expand2 = jnp.concatenate([jnp.concatenate([exp8, zero8], axis=1),
                               jnp.concatenate([zero8, exp8], axis=1)], axis=0)
    gn_ml = jnp.tile(ml_norm_g[0], 8).reshape(1, 512)
    hf, hb = _ml_mixer(main_c, main_l, gate_c, gate_l, expand2, "mlstm")

    dec2 = jnp.repeat(ret_decay_logit[0], HEAD_DIM, axis=1)
    gn_ret = jnp.tile(ret_norm_g[0], 8).reshape(1, 512)
    _, _, rf, rb = _lin_mixer(r3(ret_c, n_ctx), r3(ret_l, n_lat), dec2, ctx_out=False,
                              dk=64, hk=512, hv=512, q_off=0, k_off=512, v_off=1024,
                              lg_offs=None, q_scale=1.0, name="ret")

    lat = out_mlp(lat, [("ml", [(r2(hf), 0), (r2(hb), 0), (ml_l, 1), gn_ml, ml_skip[0].reshape(1, 512),
                                (r2(main_l), 0)]),
                        ("lin", [(r2(rf), 0), (r2(rb), 0), (ret_l, 3), gn_ret])],
                  1, lat_row, "out_mlp1_lat")
    return lat.reshape(bsz, n_lat, d)
```

```python
import functools
import math

import jax
import jax.numpy as jnp
from jax import lax
from jax.experimental import pallas as pl
from jax.experimental.pallas import tpu as pltpu

F32 = jnp.float32
BF16 = jnp.bfloat16

EPS = 1e-6
ROPE_THETA = 10000.0
GRID_W = 64
CHUNK = 64
LOG2_CHUNK = 6
BLK = 256
NCH = BLK // CHUNK
LANES = 128
GLA_TAU = 16.0
LOG2_E = 1.4426950408889634
ATTN_STREAMS = 4
MIX_NB = 2
LIN_NB = 4
PREP_BLK = 512
ATTN_AHEAD = 2
HEAD_DIM = 64
VMEM_LIMIT = 56 * 1024 * 1024


def _cparams(n_axes):
    return pltpu.CompilerParams(dimension_semantics=("arbitrary",) * n_axes,
                                vmem_limit_bytes=VMEM_LIMIT)


def _dot(a, b):
    return jnp.dot(a, b, preferred_element_type=F32)


def _dot_nt(a, b):
    return lax.dot_general(a, b, (((1,), (1,)), ((), ())), preferred_element_type=F32)


def _dot_tn(a, b):
    return lax.dot_general(a, b, (((0,), (0,)), ((), ())), preferred_element_type=F32)


def _split(x, n):
    parts = []
    for _ in range(n - 1):
        p = x.astype(BF16)
        parts.append(p)
        x = x - p.astype(F32)
    parts.append(x.astype(BF16))
    return parts


def _sigmoid(x):
    return 1.0 / (1.0 + jnp.exp(-x))


def _silu(x):
    return x * _sigmoid(x)


def _log_sigmoid(x):
    return jnp.minimum(x, 0.0) - jnp.log(1.0 + jnp.exp(-jnp.abs(x)))


def _rms(x):
    return x * lax.rsqrt(jnp.mean(x * x, axis=-1, keepdims=True) + EPS)


def _head_rms(x, ones_bd):
    ms = _dot((x * x).astype(BF16), ones_bd) * (1.0 / HEAD_DIM)
    return x * lax.rsqrt(ms + EPS)


def _rope(x, cos, sin_signed):
    w = x.shape[1]
    reps = w // LANES
    cos_w = jnp.concatenate([cos] * reps, axis=1)
    sin_w = jnp.concatenate([sin_signed] * reps, axis=1)
    lane = lax.broadcasted_iota(jnp.int32, (1, w), 1)
    partner = jnp.where((lane & 31) < 16, pltpu.roll(x, w - 16, 1), pltpu.roll(x, 16, 1))
    return x * cos_w + partner * sin_w


def _chunk_masks(rev):
    ri = lax.broadcasted_iota(jnp.int32, (BLK, BLK), 0)
    ci = lax.broadcasted_iota(jnp.int32, (BLK, BLK), 1)
    same = (ri >> LOG2_CHUNK) == (ci >> LOG2_CHUNK)
    incl = same & ((ci >= ri) if rev else (ci <= ri))
    return same, incl


def _ada_kernel(c_ref, w_ref, b_ref, o_ref):
    sc = _silu(c_ref[...]).astype(BF16)
    o_ref[...] = _dot(sc, w_ref[...].astype(BF16)) + b_ref[...]


def _ada_call(c16, ada_w, ada_b):
    depth, d, n = ada_w.shape
    tn = 1024
    return pl.pallas_call(
        _ada_kernel,
        grid=(depth, n // tn),
        in_specs=[pl.BlockSpec((16, d), lambda l, j: (0, 0)),
                  pl.BlockSpec((None, d, tn), lambda l, j: (l, 0, j)),
                  pl.BlockSpec((None, 1, tn), lambda l, j: (l, 0, j))],
        out_specs=pl.BlockSpec((None, 16, tn), lambda l, j: (l, 0, j)),
        out_shape=jax.ShapeDtypeStruct((depth, 16, n), F32),
        compiler_params=_cparams(2),
        name="ada_mod",
    )(c16, ada_w, ada_b.reshape(depth, 1, n))


def _round_robin(*gens):
    live = list(gens)
    while live:
        still = []
        for gen in live:
            try:
                next(gen)
                still.append(gen)
            except StopIteration:
                pass
        live = still


def _staggered(stages, n_rows):
    half = n_rows // 2
    first, second = stages(0, half), stages(half, half)
    next(first)
    _round_robin(second, first)


def _modulated(x_ref, g_ref, shift_ref, scale_ref, rs):
    y = _rms(x_ref[rs, :]) * g_ref[...]
    return (y * (1.0 + scale_ref[...]) + shift_ref[...]).astype(BF16)


def _inproj_ab_kernel(x_ref, g_ref, shift_ref, scale_ref, w_ref, wg_ref, bg_ref, pq_ref, pk_ref,
                      gq_ref, gk_ref, cos_ref, sin_ref, glaqk_ref, glalg_ref, glav_ref, glag_ref,
                      qt_ref, k_ref, vt_ref, *, rope):
    tm = x_ref.shape[0]

    def rows(r0, n):
        rs = slice(r0, r0 + n)
        u = _modulated(x_ref, g_ref, shift_ref, scale_ref, rs)
        yield
        res = _dot(u, w_ref[...])
        yield
        glaqk_ref[rs, :] = res[:, 0:512].astype(BF16)
        glav_ref[rs, :] = res[:, 512:1024].astype(BF16)
        glag_ref[rs, :] = res[:, 1024:1536]
        z = _dot(res[:, 1536:1664].astype(BF16), wg_ref[...]) + bg_ref[...]
        glalg_ref[rs, :] = _log_sigmoid(z) * (1.0 / GLA_TAU)
        yield
        q = _head_rms(res[:, 1664:2176], pq_ref[...]) * gq_ref[...]
        k = _head_rms(res[:, 2176:2432], pk_ref[...]) * gk_ref[...]
        if rope:
            q = _rope(q, cos_ref[rs, :], sin_ref[rs, :])
            k = _rope(k, cos_ref[rs, :], sin_ref[rs, :])
        q = q * (HEAD_DIM ** -0.5 * LOG2_E)
        yield
        for t in range(n // LANES):
            qt_ref[r0 // LANES + t] = q[t * LANES:(t + 1) * LANES, :].T.astype(BF16)
        k_ref[rs, :] = k.astype(BF16)
        vt_ref[:, rs] = res[:, 2432:2560].T.astype(BF16)

    _staggered(rows, tm)


def _inproj_cd_kernel(x_ref, g_ref, shift_ref, scale_ref, w_ref, cos_ref, sin_ref,
                      ml_ref, ret_ref, retv_ref, retg_ref, *, rope):
    tm = x_ref.shape[0]

    def rows(r0, n):
        rs = slice(r0, r0 + n)
        u = _modulated(x_ref, g_ref, shift_ref, scale_ref, rs)
        yield
        res = _dot(u, w_ref[...])
        yield
        ml_ref[rs, :] = res[:, 0:1024]
        q = res[:, 1024:1536]
        k = res[:, 1536:2048]
        if rope:
            q = _rope(q, cos_ref[rs, :], sin_ref[rs, :])
            k = _rope(k, cos_ref[rs, :], sin_ref[rs, :])
        ret_ref[rs, 0:512] = (q * HEAD_DIM ** -0.5).astype(BF16)
        ret_ref[rs, 512:1024] = k.astype(BF16)
        retv_ref[rs, :] = res[:, 2048:2560].astype(BF16)
        retg_ref[rs, :] = res[:, 2560:3072]

    _staggered(rows, tm)


def _mod_spec(layer, col, row_of):
    return pl.BlockSpec((None, None, 1, 1024), lambda i: (layer, row_of(i), 0, col))


def _const_spec(shape):
    nd = len(shape)
    return pl.BlockSpec(shape, lambda *_: (0,) * nd)


def _row_out(rows, width, dtype, tm):
    return jax.ShapeDtypeStruct((rows, width), dtype), pl.BlockSpec((tm, width), lambda i: (i, 0))


def _inproj_call(kernel, x2d, mod4, g4, layer, row_of, weights, rope_tabs, outs, tm, name):
    rows, d = x2d.shape
    n_pos = rope_tabs[0].shape[0] // tm
    in_specs = [pl.BlockSpec((tm, d), lambda i: (i, 0)),
                pl.BlockSpec((None, None, 1, d), lambda i: (layer, 0, 0, 0)),
                _mod_spec(layer, 0, row_of), _mod_spec(layer, 1, row_of)]
    in_specs += [_const_spec(w.shape) for w in weights]
    in_specs += [pl.BlockSpec((tm, LANES), lambda i: (i % n_pos, 0))] * 2
    return pl.pallas_call(
        kernel,
        grid=(rows // tm,),
        in_specs=in_specs,
        out_specs=[spec for _, spec in outs],
        out_shape=[shape for shape, _ in outs],
        compiler_params=_cparams(1),
        name=name,
    )(x2d, g4, mod4, mod4, *weights, *rope_tabs)


def _lin_block(q, k, vb, lg, st_ref, *, rev, dk, const_decay):
    n_grp = q.shape[1] // LANES
    hpg = LANES // dk
    vw = hpg * HEAD_DIM
    _, incl = _chunk_masks(rev)
    if const_decay:
        pos = lax.broadcasted_iota(jnp.int32, (BLK, 1), 0) & (CHUNK - 1)
        steps = ((CHUNK - pos) if rev else (pos + 1)).astype(F32)
        b = steps * lg
        r = float(CHUNK) * lg - b
        totals = [float(CHUNK) * lg] * NCH
    else:
        t_incl = jnp.where(incl, 1.0, 0.0).astype(BF16)
        b = None
        for part in _split(lg, 2):
            pb = _dot(t_incl, part)
            b = pb if b is None else b + pb
        last = 0 if rev else CHUNK - 1
        totals = [b[c * CHUNK + last:c * CHUNK + last + 1, :] for c in range(NCH)]
        r = jnp.concatenate([totals[c] - b[c * CHUNK:(c + 1) * CHUNK, :] for c in range(NCH)], axis=0)
    yield
    qd = (q * jnp.exp(b)).astype(BF16)
    ki = (k * jnp.exp(-b)).astype(BF16)
    kd = (k * jnp.exp(r)).astype(BF16)
    yield

    e_i = lax.broadcasted_iota(jnp.int32, (vw, LANES), 0)
    d_i = lax.broadcasted_iota(jnp.int32, (vw, LANES), 1)
    bd_mask = jnp.where((e_i // HEAD_DIM) == (d_i // dk), 1.0, 0.0)
    key_mask = bd_mask.astype(BF16)
    r_i = lax.broadcasted_iota(jnp.int32, (vw, vw), 0)
    c_i = lax.broadcasted_iota(jnp.int32, (vw, vw), 1)
    val_mask = jnp.where((r_i // CHUNK) == (c_i // HEAD_DIM), 1.0, 0.0).astype(BF16)
    q_pos = lax.broadcasted_iota(jnp.int32, (CHUNK, vw), 0)
    k_pos = lax.broadcasted_iota(jnp.int32, (CHUNK, vw), 1) & (CHUNK - 1)
    causal = (k_pos >= q_pos) if rev else (k_pos <= q_pos)

    order = list(range(NCH - 1, -1, -1) if rev else range(NCH))
    rows = lambda c: slice(c * CHUNK, (c + 1) * CHUNK)
    keys = lambda g: slice(g * LANES, (g + 1) * LANES)
    vals = lambda g: slice(g * vw, (g + 1) * vw)

    scores, incs = {}, {}
    for c in order:
        for g in range(n_grp):
            keys_bd = jnp.concatenate([ki[rows(c), keys(g)]] * hpg, axis=0) * key_mask
            scores[c, g] = _dot_nt(qd[rows(c), keys(g)], keys_bd)
            incs[c, g] = _dot_tn(vb[rows(c), vals(g)], kd[rows(c), keys(g)])
            yield

    rows_out = [None] * NCH
    for c in order:
        decay = jnp.exp(totals[c])
        outs = []
        for g in range(n_grp):
            st = st_ref[g]
            inter = _dot_nt(qd[rows(c), keys(g)], st.astype(BF16))
            st_ref[g] = st * decay[:, keys(g)] + incs[c, g] * bd_mask
            p = jnp.where(causal, scores[c, g], 0.0).astype(BF16)
            vals_bd = jnp.concatenate([vb[rows(c), vals(g)]] * hpg, axis=0) * val_mask
            outs.append(inter + _dot(p, vals_bd))
            yield
        rows_out[c] = jnp.concatenate(outs, axis=1)
    return jnp.concatenate(rows_out, axis=0)


def _lin_dual_kernel(*refs, ctx_out, dk, hk, const_decay, q_scale):
    it = iter(refs)
    qk_c, v_c = next(it), next(it)
    lg_c = (None, None) if const_decay else (next(it), next(it))
    qk_f, v_f = next(it), next(it)
    lg_f = None if const_decay else next(it)
    qk_b, v_b = next(it), next(it)
    lg_b = None if const_decay else next(it)
    dec_ref = next(it) if const_decay else None
    of_ctx_ref, ob_ctx_ref = (next(it), next(it)) if ctx_out else (None, None)
    of_lat_ref, ob_lat_ref, stf_ref, stb_ref = next(it), next(it), next(it), next(it)
    i = pl.program_id(1)

    def run(j, qk_ref, v_ref, lg_ref, o_ref, st_ref, rev):
        d = 1 if rev else 0
        q = qk_ref[j, :, 0:hk].astype(F32) * q_scale
        k = qk_ref[j, :, hk:2 * hk].astype(F32)
        lg = _log_sigmoid(dec_ref[d:d + 1, :]) if const_decay else lg_ref[j]
        o = yield from _lin_block(q, k, v_ref[j], lg, st_ref.at[j], rev=rev, dk=dk, const_decay=const_decay)
        if o_ref is not None:
            o_ref[j] = o

    def both_directions(fwd, bwd):
        _round_robin(*[run(j, *refs, st, rev) for j in range(stf_ref.shape[0])
                       for refs, st, rev in ((fwd, stf_ref, False), (bwd, stb_ref, True))])

    @pl.when(i == 0)
    def _():
        stf_ref[...] = jnp.zeros(stf_ref.shape, F32)
        stb_ref[...] = jnp.zeros(stb_ref.shape, F32)
        both_directions((qk_c, v_c, lg_c[0], of_ctx_ref), (qk_c, v_c, lg_c[1], ob_ctx_ref))

    @pl.when(i > 0)
    def _():
        both_directions((qk_f, v_f, lg_f, of_lat_ref), (qk_b, v_b, lg_b, ob_lat_ref))


def _fwd_blk(i):
    return jnp.maximum(i, 1) - 1


def _bwd_blk(i, n_blk):
    return n_blk - jnp.maximum(i, 1)


def _seq_specs(n_blk, width, lane_blk=0, nb=MIX_NB):
    return (pl.BlockSpec((nb, BLK, width), lambda b, i: (b, 0, lane_blk)),
            pl.BlockSpec((nb, BLK, width), lambda b, i: (b, _fwd_blk(i), lane_blk)),
            pl.BlockSpec((nb, BLK, width), lambda b, i: (b, _bwd_blk(i, n_blk), lane_blk)))


def _lin_mixer(ctx, lat, dec2, *, ctx_out, dk, hk, hv, q_scale, name):
    (qk_c, v_c, lg_c), (qk_l, v_l, lg_l) = ctx, lat
    const_decay = dec2 is not None
    bsz, n_lat, _ = qk_l.shape
    n_blk = n_lat // BLK
    qk_specs = _seq_specs(n_blk, 2 * hk, 0, LIN_NB)
    v_specs = _seq_specs(n_blk, hv, 0, LIN_NB)
    lgf_specs, lgb_specs = _seq_specs(n_blk, hk, 0, LIN_NB), _seq_specs(n_blk, hk, 1, LIN_NB)
    octx_spec, olatf_spec, olatb_spec = v_specs
    args, in_specs = [qk_c, v_c], [qk_specs[0], v_specs[0]]
    if not const_decay:
        args += [lg_c, lg_c]
        in_specs += [lgf_specs[0], lgb_specs[0]]
    for which, lg_specs in ((1, lgf_specs), (2, lgb_specs)):
        args += [qk_l, v_l]
        in_specs += [qk_specs[which], v_specs[which]]
        if not const_decay:
            args.append(lg_l)
            in_specs.append(lg_specs[which])
    if const_decay:
        args.append(dec2)
        in_specs.append(_const_spec(dec2.shape))
    out_specs, out_shape = [], []
    if ctx_out:
        out_specs += [octx_spec, octx_spec]
        out_shape += [jax.ShapeDtypeStruct((bsz, BLK, hv), F32)] * 2
    out_specs += [olatf_spec, olatb_spec]
    out_shape += [jax.ShapeDtypeStruct((bsz, n_lat, hv), F32)] * 2
    n_grp, vw = hk // LANES, (LANES // dk) * HEAD_DIM
    kern = functools.partial(_lin_dual_kernel, ctx_out=ctx_out, dk=dk, hk=hk, const_decay=const_decay,
                             q_scale=q_scale)
    outs = pl.pallas_call(
        kern,
        grid=(bsz // LIN_NB, n_blk + 1),
        in_specs=in_specs,
        out_specs=out_specs,
        out_shape=out_shape,
        scratch_shapes=[pltpu.VMEM((LIN_NB, n_grp, vw, LANES), F32)] * 2,
        compiler_params=_cparams(2),
        name=name,
    )(*args)
    return tuple(outs) if ctx_out else (None, None) + tuple(outs)


def _attn_kernel(q_ref, kc_ref, vc_ref, kl_ref, vl_ref, o_ref, *, tk, n_lat_grp):
    n_tiles = q_ref.shape[0]
    is_a = lax.broadcasted_iota(jnp.int32, (LANES, 1), 0) < HEAD_DIM
    n_acc = HEAD_DIM + 16

    def col_max(blocks):
        mx = blocks[0]
        for blk in blocks[1:]:
            mx = jnp.maximum(mx, blk)
        return jnp.max(mx, axis=0, keepdims=True)

    n_sub = tk // LANES
    n_ctx_sub = kc_ref.shape[0] // LANES
    values = [vc_ref[...]] + [vl_ref[:, g * tk:(g + 1) * tk] for g in range(n_lat_grp)]
    n_streams = math.gcd(n_tiles, ATTN_STREAMS)
    streams = range(n_streams)

    def tiles(j, carry):
        ws = []
        for t in streams:
            qt = q_ref[j * n_streams + t]
            zero = jnp.zeros_like(qt)
            ws.append(jnp.concatenate([jnp.where(is_a, qt, zero), jnp.where(is_a, zero, qt)], axis=1))
        ms = [jnp.full((1, 2 * LANES), -jnp.inf, F32) for _ in streams]
        accs = [jnp.zeros((n_acc, 2 * LANES), F32) for _ in streams]

        def lat_scores(t, g, i):
            lo = g * tk + i * LANES
            return _dot(kl_ref[lo:lo + LANES, :], ws[t])

        groups = [[[_dot(kc_ref[i * LANES:(i + 1) * LANES, :], ws[t]) for i in range(n_ctx_sub)]]
                  for t in streams]
        for g in range(min(ATTN_AHEAD - 1, n_lat_grp)):
            for t in streams:
                groups[t].append([lat_scores(t, g, i) for i in range(n_sub)])
        for g in range(n_lat_grp + 1):
            issue = g + ATTN_AHEAD - 1
            n_cur = len(groups[0][g])
            m_new = [jnp.maximum(ms[t], col_max(groups[t][g])) for t in streams]
            nxt = [[] for _ in streams]
            ps = [[] for _ in streams]
            for i in range(max(n_cur, n_sub)):
                for t in streams:
                    if issue < n_lat_grp and i < n_sub:
                        nxt[t].append(lat_scores(t, issue, i))
                    if i < n_cur:
                        ps[t].append(jnp.exp2(groups[t][g][i] - m_new[t]).astype(BF16))
            vaug = jnp.concatenate([values[g], jnp.ones((16, n_cur * LANES), BF16)], axis=0)
            for t in streams:
                if nxt[t]:
                    groups[t].append(nxt[t])
                p = jnp.concatenate(ps[t], axis=0)
                accs[t] = accs[t] * jnp.exp2(ms[t] - m_new[t]) + _dot(vaug, p)
                ms[t] = m_new[t]
        for t in streams:
            o = accs[t][0:HEAD_DIM, :] / accs[t][HEAD_DIM:HEAD_DIM + 1, :]
            ot = jnp.concatenate([o[:, 0:LANES], o[:, LANES:2 * LANES]], axis=0).T
            row0 = pl.multiple_of((j * n_streams + t) * LANES, LANES)
            o_ref[pl.ds(row0, LANES), :] = ot.astype(o_ref.dtype)
        return carry

    lax.fori_loop(0, n_tiles // n_streams, tiles, 0)


def _attn_call(qt, kc, vc, kl, vl, *, tk, with_lat, name):
    bsz, n_tiles, width, _ = qt.shape
    kv_of = lambda p: p // 2
    k_spec = lambda a: pl.BlockSpec((None, a.shape[1], LANES), lambda b, p: (b, 0, kv_of(p)))
    v_spec = lambda a: pl.BlockSpec((None, HEAD_DIM, a.shape[2]), lambda b, p: (b, kv_of(p), 0))
    kern = functools.partial(_attn_kernel, tk=tk, n_lat_grp=(kl.shape[1] // tk) if with_lat else 0)
    return pl.pallas_call(
        kern,
        grid=(bsz, width // LANES),
        in_specs=[pl.BlockSpec((None, n_tiles, LANES, LANES), lambda b, p: (b, 0, p, 0)),
                  k_spec(kc), v_spec(vc), k_spec(kl), v_spec(vl)],
        out_specs=pl.BlockSpec((None, n_tiles * LANES, LANES), lambda b, p: (b, 0, p)),
        out_shape=jax.ShapeDtypeStruct((bsz, n_tiles * LANES, width), BF16),
        compiler_params=_cparams(2),
        name=name,
    )(qt, kc, vc, kl, vl)


def _ml_prep_kernel(x_ref, prev_ref, next_ref, cw_ref, cb_ref, wq_ref, wk_ref, wv_ref, wg_ref, bg_ref,
                    xc_ref, qkv_ref, gate_ref):
    i, n = pl.program_id(1), pl.num_programs(1)
    rows, d = x_ref.shape
    x = x_ref[...]
    row = lax.broadcasted_iota(jnp.int32, (rows, 1), 0)
    before = prev_ref[7:8, :] * jnp.where(i > 0, 1.0, 0.0)
    after = next_ref[0:1, :] * jnp.where(i < n - 1, 1.0, 0.0)
    x_prev = jnp.where(row == 0, before, pltpu.roll(x, 1, 0))
    x_next = jnp.where(row == rows - 1, after, pltpu.roll(x, rows - 1, 0))
    xc = _silu(cw_ref[0:1, :] * x_prev + cw_ref[1:2, :] * x + cw_ref[2:3, :] * x_next + cb_ref[...])
    xcb = xc.astype(BF16)
    q = _dot(xcb, wq_ref[...])
    k = _dot(xcb, wk_ref[...])
    v = _dot(x.astype(BF16), wv_ref[...])
    xc_ref[...] = xc
    gin = jnp.concatenate([q, k, v], axis=1).astype(BF16)
    qkv_ref[...] = gin
    gate_ref[...] = _dot_nt(wg_ref[...], gin) + bg_ref[...]


def _ml_prep_call(ml, conv_w, conv_b, wq, wk, wv, wg_t, bg_t, name):
    bsz, n_tok, _ = ml.shape
    d = conv_w.shape[1]
    blk = min(n_tok, PREP_BLK)
    n_blk, hb = n_tok // blk, blk // 8
    consts = [conv_w, conv_b, wq, wk, wv, wg_t, bg_t]
    tok_out = lambda width, dtype: (jax.ShapeDtypeStruct((bsz, n_tok, width), dtype),
                                    pl.BlockSpec((None, blk, width), lambda b, i: (b, i, 0)))
    outs = [tok_out(d, F32), tok_out(3 * d, BF16),
            (jax.ShapeDtypeStruct((bsz, 32, n_tok), F32), pl.BlockSpec((None, 32, blk), lambda b, i: (b, 0, i)))]
    return pl.pallas_call(
        _ml_prep_kernel,
        grid=(bsz, n_blk),
        in_specs=[pl.BlockSpec((None, blk, d), lambda b, i: (b, i, 0)),
                  pl.BlockSpec((None, 8, d), lambda b, i: (b, jnp.maximum(i * hb - 1, 0), 0)),
                  pl.BlockSpec((None, 8, d), lambda b, i: (b, jnp.minimum((i + 1) * hb, n_blk * hb - 1), 0))]
                 + [_const_spec(c.shape) for c in consts],
        out_specs=[spec for _, spec in outs],
        out_shape=[shape for shape, _ in outs],
        compiler_params=_cparams(2),
        name=name,
    )(ml, ml, ml, *consts)


def _ml_block(q, k, v, ig, lf, st_ref, m_ref, expand_ref, *, rev):
    n_heads, d_all = ig.shape[0], q.shape[1]
    n_grp = d_all // LANES
    same, incl = _chunk_masks(rev)
    ri = lax.broadcasted_iota(jnp.int32, (BLK, BLK), 0)
    ci = lax.broadcasted_iota(jnp.int32, (BLK, BLK), 1)
    t_pre = jnp.where(same & ((ri >= ci) if rev else (ri <= ci)), 1.0, 0.0).astype(BF16)
    t_same = jnp.where(same, 1.0, 0.0).astype(BF16)
    zeros8 = jnp.zeros((8, BLK), F32)
    brow, bend = None, None
    for part in _split(jnp.concatenate([lf, zeros8], axis=0), 3):
        pb, pe = _dot(part, t_pre), _dot(part, t_same)
        brow = pb if brow is None else brow + pb
        bend = pe if bend is None else bend + pe
    yield
    brow, bend = brow[0:n_heads], bend[0:n_heads]
    a_row = bend - brow + ig
    ncr = ig - brow

    lane = lax.broadcasted_iota(jnp.int32, (1, BLK), 1)
    chunk_of = lane >> LOG2_CHUNK
    pos = lane & (CHUNK - 1)
    neg_inf = jnp.float32(-jnp.inf)

    m_s = m_ref[:, 0:1]
    mst_row = jnp.zeros((n_heads, BLK), F32)
    mloc_row = jnp.zeros((n_heads, BLK), F32)
    s_old, s_new = [None] * NCH, [None] * NCH
    for c in (range(NCH - 1, -1, -1) if rev else range(NCH)):
        in_c = chunk_of == c
        bend_c = jnp.max(jnp.where(in_c, bend, neg_inf), axis=1, keepdims=True)
        mloc_c = jnp.max(jnp.where(in_c, a_row, neg_inf), axis=1, keepdims=True)
        m_new = jnp.maximum(bend_c + m_s, mloc_c)
        s_old[c] = jnp.exp(bend_c + m_s - m_new)
        s_new[c] = jnp.exp(mloc_c - m_new)
        mst_row = jnp.where(in_c, m_s, mst_row)
        mloc_row = jnp.where(in_c, mloc_c, mloc_row)
        m_s = m_new
    m_ref[...] = jnp.broadcast_to(m_s, m_ref.shape)

    cm = ncr
    for sh in (1, 2, 4, 8, 16, 32):
        if rev:
            shifted, valid = pltpu.roll(cm, BLK - sh, 1), pos < CHUNK - sh
        else:
            shifted, valid = pltpu.roll(cm, sh, 1), pos >= sh
        cm = jnp.where(valid, jnp.maximum(cm, shifted), cm)
    g_row = jnp.maximum(mst_row, cm)
    w_row = jnp.exp(a_row - mloc_row)
    sint_row = jnp.exp(mst_row - g_row)
    floor_row = jnp.exp(-(brow + g_row))

    n_q = 4 * n_heads
    eye = jnp.where(lax.broadcasted_iota(jnp.int32, (n_q, LANES), 0)
                    == lax.broadcasted_iota(jnp.int32, (n_q, LANES), 1), 1.0, 0.0).astype(BF16)
    yield
    cols = None
    for part in _split(jnp.concatenate([g_row, floor_row, w_row, sint_row], axis=0), 3):
        term = _dot_tn(part, eye)
        cols = term if cols is None else cols + term
    yield
    lane_l = lax.broadcasted_iota(jnp.int32, (1, LANES), 1)
    per_group = LANES // HEAD_DIM

    def col(quantity, h):
        return cols[:, n_heads * quantity + h:n_heads * quantity + h + 1]

    def expanded(quantity):
        groups = []
        for g in range(n_grp):
            grp = col(quantity, per_group * g + per_group - 1)
            for hh in range(per_group - 2, -1, -1):
                grp = jnp.where(lane_l < (hh + 1) * HEAD_DIM, col(quantity, per_group * g + hh), grp)
            groups.append(jnp.broadcast_to(grp, (BLK, LANES)))
        return jnp.concatenate(groups, axis=1)

    floor_exp = expanded(1)
    expand8 = expand_ref[...]
    yield
    w_exp = _dot_tn(jnp.concatenate([w_row, zeros8], axis=0).astype(BF16), expand8)
    sint_exp = None
    for part in _split(jnp.concatenate([sint_row, zeros8], axis=0), 2):
        term = _dot_tn(part, expand8)
        sint_exp = term if sint_exp is None else sint_exp + term
    yield

    lane128 = lax.broadcasted_iota(jnp.int32, (1, LANES), 1)
    ss = jnp.zeros((n_heads, LANES), F32)
    for c in range(NCH):
        ss = jnp.where(lane128 == c, s_old[c], ss)
        ss = jnp.where(lane128 == NCH + c, s_new[c], ss)
    ss = jnp.concatenate([ss, jnp.zeros((8, LANES), F32)], axis=0)
    scal = None
    for part in _split(ss, 3):
        term = _dot_tn(part, expand8)
        scal = term if scal is None else scal + term

    qb, kb, vb = q.astype(BF16), k.astype(BF16), v.astype(BF16)
    kw = (k * w_exp).astype(BF16)
    ones_chunk = jnp.ones((CHUNK, LANES), BF16)

    e_i = lax.broadcasted_iota(jnp.int32, (2 * LANES, LANES), 0)
    d_i = lax.broadcasted_iota(jnp.int32, (2 * LANES, LANES), 1)
    bd_mask = jnp.where(((e_i & (LANES - 1)) // HEAD_DIM) == (d_i // HEAD_DIM), 1.0, 0.0)
    key_mask = bd_mask[0:LANES].astype(BF16)
    r_i = lax.broadcasted_iota(jnp.int32, (LANES, 2 * LANES), 0)
    c_i = lax.broadcasted_iota(jnp.int32, (LANES, 2 * LANES), 1)
    val_mask = jnp.where((r_i // CHUNK) == ((c_i & (LANES - 1)) // HEAD_DIM), 1.0, 0.0).astype(BF16)
    q_pos = lax.broadcasted_iota(jnp.int32, (CHUNK, LANES), 0)
    k_pos = lax.broadcasted_iota(jnp.int32, (CHUNK, LANES), 1) & (CHUNK - 1)
    causal = (k_pos >= q_pos) if rev else (k_pos <= q_pos)
    first_head = lane_l < HEAD_DIM
    ncr_sw = jnp.concatenate([pltpu.roll(ncr[:, p * LANES:(p + 1) * LANES], HEAD_DIM, 1)
                              for p in range(BLK // LANES)], axis=1)

    order = list(range(NCH - 1, -1, -1) if rev else range(NCH))
    rows = lambda c: slice(c * CHUNK, (c + 1) * CHUNK)
    grp = lambda g: slice(g * LANES, (g + 1) * LANES)

    def key_bias(c, g):
        pair = slice((c // 2) * LANES, (c // 2 + 1) * LANES)
        h0, h1 = 2 * g, 2 * g + 1
        if c % 2 == 0:
            return jnp.where(first_head, ncr[h0:h0 + 1, pair], ncr_sw[h1:h1 + 1, pair])
        return jnp.where(first_head, ncr_sw[h0:h0 + 1, pair], ncr[h1:h1 + 1, pair])

    scores, incs = {}, {}
    for c in order:
        for g in range(n_grp):
            keys_bd = jnp.concatenate([kb[rows(c), grp(g)]] * 2, axis=0) * key_mask
            scores[c, g] = _dot_nt(qb[rows(c), grp(g)], keys_bd)
            vaug = jnp.concatenate([vb[rows(c), grp(g)], ones_chunk], axis=1)
            incs[c, g] = _dot_tn(vaug, kw[rows(c), grp(g)])
            yield

    out_rows = [None] * NCH
    for c in order:
        outs = []
        for g in range(n_grp):
            st = st_ref[g]
            oaug = _dot_nt(qb[rows(c), grp(g)], st.astype(BF16))
            st_ref[g] = (st * scal[c:c + 1, grp(g)]
                         + incs[c, g] * bd_mask * scal[NCH + c:NCH + c + 1, grp(g)])
            g_cols = jnp.where(first_head, col(0, 2 * g)[rows(c)], col(0, 2 * g + 1)[rows(c)])
            e = jnp.exp(jnp.where(causal, key_bias(c, g) - g_cols, neg_inf))
            p = (scores[c, g] * e).astype(BF16)
            vaug = jnp.concatenate([vb[rows(c), grp(g)], ones_chunk], axis=1)
            vals_bd = jnp.concatenate([vaug] * 2, axis=0) * val_mask
            nd = _dot(p, vals_bd)
            sint = sint_exp[rows(c), grp(g)]
            num = nd[:, 0:LANES] + sint * oaug[:, 0:LANES]
            den = nd[:, LANES:2 * LANES] + sint * oaug[:, LANES:2 * LANES]
            outs.append(num / jnp.maximum(jnp.abs(den), floor_exp[rows(c), grp(g)]))
            yield
        out_rows[c] = jnp.concatenate(outs, axis=1)
    return jnp.concatenate(out_rows, axis=0)


def _ml_dual_kernel(qkv_c, g_c, qkv_f, g_f, qkv_b, g_b, expand_ref,
                    hf_ref, hb_ref, stf_ref, mf_ref, stb_ref, mb_ref):
    d = hf_ref.shape[-1]
    i = pl.program_id(1)

    def run(j, qkv_ref, gate_ref, o_ref, st_ref, m_ref, rev):
        g0 = 16 if rev else 0
        q = qkv_ref[j, :, 0:d] * HEAD_DIM ** -0.5
        k = qkv_ref[j, :, d:2 * d].astype(F32)
        v = qkv_ref[j, :, 2 * d:3 * d]
        ig = gate_ref[j, g0:g0 + 8, :]
        lf = _log_sigmoid(gate_ref[j, g0 + 8:g0 + 16, :])
        h = yield from _ml_block(q, k, v, ig, lf, st_ref.at[j], m_ref.at[j], expand_ref, rev=rev)
        if o_ref is not None:
            o_ref[j] = h

    def both_directions(fwd, bwd):
        _round_robin(*[run(j, *refs, st, m, rev) for j in range(MIX_NB)
                       for refs, st, m, rev in ((fwd, stf_ref, mf_ref, False), (bwd, stb_ref, mb_ref, True))])

    @pl.when(i == 0)
    def _():
        for ref in (stf_ref, mf_ref, stb_ref, mb_ref):
            ref[...] = jnp.zeros(ref.shape, F32)
        both_directions((qkv_c, g_c, None), (qkv_c, g_c, None))

    @pl.when(i > 0)
    def _():
        both_directions((qkv_f, g_f, hf_ref), (qkv_b, g_b, hb_ref))


def _ml_mixer(ctx, lat, expand2, name):
    bsz, n_lat, width = lat[0].shape
    d = width // 3
    n_blk = n_lat // BLK
    qkv_specs, k_specs = _seq_specs(n_blk, width), _seq_specs(n_blk, d)
    gate_specs = [pl.BlockSpec((MIX_NB, 32, BLK), lambda b, i: (b, 0, 0)),
                  pl.BlockSpec((MIX_NB, 32, BLK), lambda b, i: (b, 0, _fwd_blk(i))),
                  pl.BlockSpec((MIX_NB, 32, BLK), lambda b, i: (b, 0, _bwd_blk(i, n_blk)))]
    args, in_specs = [], []
    for which, src in ((0, ctx), (1, lat), (2, lat)):
        args += list(src)
        in_specs += [qkv_specs[which], gate_specs[which]]
    state = [pltpu.VMEM((MIX_NB, d // LANES, 2 * LANES, LANES), F32), pltpu.VMEM((MIX_NB, 8, LANES), F32)]
    return pl.pallas_call(
        _ml_dual_kernel,
        grid=(bsz // MIX_NB, n_blk + 1),
        in_specs=in_specs + [_const_spec(expand2.shape)],
        out_specs=[k_specs[1], k_specs[2]],
        out_shape=[jax.ShapeDtypeStruct((bsz, n_lat, d), F32)] * 2,
        scratch_shapes=state + state,
        compiler_params=_cparams(2),
        name=name,
    )(*args, expand2)


N_MIXER_REFS = {"raw": 1, "lin": 4, "ml": 6}


def _mixer_half(refs, mode, ones_bd, rs):
    if mode == "raw":
        return refs[0][rs, :]
    fwd_ref, bwd_ref, gate_ref, gn_ref = refs[:4]
    o = _head_rms(fwd_ref[rs, :] + bwd_ref[rs, :], ones_bd) * gn_ref[...]
    if mode == "ml":
        skip_ref, xc_ref = refs[4:]
        o = o + skip_ref[...] * xc_ref[rs, :]
    return (o * _silu(gate_ref[rs, :])).astype(BF16)


def _out_mlp_kernel(*refs, modes):
    x_ref, ones_ref = refs[:2]
    na, nb = N_MIXER_REFS[modes[0]], N_MIXER_REFS[modes[1]]
    a_refs, b_refs = refs[2:2 + na], refs[2 + na:2 + na + nb]
    (gate1_ref, shift2_ref, scale2_ref, gate2_ref, g1_ref, g2_ref, g3_ref,
     woa_ref, wob_ref, w1_ref, w2_ref, o_ref) = refs[2 + na + nb:]
    tm, d = x_ref.shape

    def rows(r0, n):
        rs = slice(r0, r0 + n)
        ya = _mixer_half(a_refs, modes[0], ones_ref[...], rs)
        yield
        yb = _mixer_half(b_refs, modes[1], ones_ref[...], rs)
        yield
        t = _dot(ya, woa_ref[...]) + _dot(yb, wob_ref[...])
        yield
        lat = x_ref[rs, :] + gate1_ref[...] * (_rms(t) * g1_ref[...])
        u = (_rms(lat) * g2_ref[...] * (1.0 + scale2_ref[...]) + shift2_ref[...]).astype(BF16)
        yield
        acc = None
        for c in range(w1_ref.shape[1] // d):
            h = _dot(u, w1_ref[:, c * d:(c + 1) * d])
            yield
            h = jnp.square(jnp.maximum(h, 0.0)).astype(BF16)
            term = _dot(h, w2_ref[c * d:(c + 1) * d, :])
            acc = term if acc is None else acc + term
            yield
        o_ref[rs, :] = lat + gate2_ref[...] * (_rms(acc) * g3_ref[...])

    _staggered(rows, tm)


def _out_mlp_call(x2d, halves, ones_bd, mod4, g4, layer, row_of, woa, wob, w1, w2, tm, name):
    rows, d = x2d.shape
    half = woa.shape[0]
    g_spec = lambda j: pl.BlockSpec((None, None, 1, d), lambda i: (layer, j, 0, 0))
    single = dict(pipeline_mode=pl.Buffered(1))
    w_spec = lambda w: pl.BlockSpec(w.shape, lambda i: (0, 0), **single)
    args, in_specs = [x2d, ones_bd], [pl.BlockSpec((tm, d), lambda i: (i, 0)), _const_spec(ones_bd.shape)]
    for _, operands in halves:
        for op in operands:
            if isinstance(op, tuple):
                arr, blk = op
                assert arr.shape[0] == rows
                args.append(arr)
                in_specs.append(pl.BlockSpec((tm, half), lambda i, blk=blk: (i, blk)))
            else:
                args.append(op)
                in_specs.append(_const_spec(op.shape))
    args += [mod4, mod4, mod4, mod4, g4, g4, g4, woa, wob, w1, w2]
    in_specs += [_mod_spec(layer, 2, row_of), _mod_spec(layer, 3, row_of),
                 _mod_spec(layer, 4, row_of), _mod_spec(layer, 5, row_of),
                 g_spec(1), g_spec(2), g_spec(3),
                 w_spec(woa), w_spec(wob), w_spec(w1), w_spec(w2)]
    return pl.pallas_call(
        functools.partial(_out_mlp_kernel, modes=tuple(m for m, _ in halves)),
        grid=(rows // tm,),
        in_specs=in_specs,
        out_specs=pl.BlockSpec((tm, d), lambda i: (i, 0)),
        out_shape=jax.ShapeDtypeStruct((rows, d), F32),
        compiler_params=_cparams(1),
        name=name,
    )(*args)


def _rope_tables(n_tokens):
    n_rows = n_tokens // GRID_W
    rows, cols = jnp.meshgrid(jnp.arange(n_rows), jnp.arange(GRID_W), indexing="ij")
    axis_dim = HEAD_DIM // 2
    inv_freq = ROPE_THETA ** (-jnp.arange(0, axis_dim, 2, dtype=F32) / axis_dim)
    ang_r = rows.reshape(-1, 1).astype(F32) * inv_freq
    ang_c = cols.reshape(-1, 1).astype(F32) * inv_freq
    cos = jnp.concatenate([jnp.cos(ang_r)] * 2 + [jnp.cos(ang_c)] * 2, axis=1)
    sin = jnp.concatenate([-jnp.sin(ang_r), jnp.sin(ang_r), -jnp.sin(ang_c), jnp.sin(ang_c)], axis=1)
    return jnp.tile(cos, (1, 2)), jnp.tile(sin, (1, 2))


def _group_ones(n, group=HEAD_DIM):
    idx = jnp.arange(n) // group
    return (idx[:, None] == idx[None, :]).astype(BF16)


def _block_diag(w):
    n_h, d, e = w.shape
    eye = jnp.eye(n_h, dtype=w.dtype)
    return (eye[:, None, :, None] * w[:, :, None, :]).reshape(n_h * d, n_h * e)


def kernel(x, c, ctx, c_ctx, ada_w, ada_b, norm_g, w_out, mlp_w1, mlp_w2, ab_w_in, gla_w_gate, gla_b_gate,
           gla_norm_g, att_qk_norm_g, cd_w_in, ml_conv_w, ml_conv_b, ml_w_qkv, ml_w_gate, ml_b_gate, ml_norm_g,
           ml_skip, ret_decay_logit, ret_norm_g):
    bsz, n_lat, d = x.shape
    n_ctx = ctx.shape[1]
    depth = ada_w.shape[0]
    assert n_ctx == BLK and n_lat % BLK == 0 and depth == 2 and d == 1024
    tm = 512
    assert n_lat % tm == 0 and (bsz * n_ctx) % tm == 0 and bsz % MIX_NB == 0 and bsz % LIN_NB == 0
    lat_row = lambda i: i // (n_lat // tm)
    ctx_row = lambda i: bsz

    c16 = jnp.concatenate([c, c_ctx[None], jnp.zeros((16 - bsz - 1, d), F32)], axis=0)
    mod4 = _ada_call(c16, ada_w, ada_b).reshape(depth, 16, 1, 6 * d)
    g4 = norm_g.reshape(depth, 4, 1, d)
    cos, sin = _rope_tables(n_lat)
    ones512, ones256 = _group_ones(512), _group_ones(256)

    lat = x.reshape(bsz * n_lat, d)
    cx = ctx.reshape(bsz * n_ctx, d)

    w = ab_w_in[0]
    wk, wv = w[:, 2080:2208], w[:, 2208:2336]
    dup = lambda m: jnp.concatenate([m[:, :64], m[:, :64], m[:, 64:], m[:, 64:]], axis=1)
    w0 = jnp.concatenate([w[:, :1568], jnp.zeros((d, 96), F32), w[:, 1568:2080], dup(wk), wv],
                         axis=1).astype(BF16)
    wg = jnp.zeros((LANES, 512), F32)
    wg = wg.at[0:16, 0:256].set(gla_w_gate[0, 0]).at[16:32, 256:512].set(gla_w_gate[0, 1]).astype(BF16)
    bg = gla_b_gate[0].reshape(1, 512)
    gq = jnp.tile(att_qk_norm_g[0, 0], 8).reshape(1, 512)
    gk = jnp.tile(att_qk_norm_g[0, 1], 4).reshape(1, 256)
    weights0 = [w0, wg, bg, ones512, ones256, gq, gk]

    def inproj0(x2d, n_tok, tile, row_of, rope, name):
        rows = x2d.shape[0]
        per_b = n_tok // tile
        qt_out = (jax.ShapeDtypeStruct((bsz, n_tok // LANES, 512, LANES), BF16),
                  pl.BlockSpec((None, tile // LANES, 512, LANES), lambda i: (i // per_b, i % per_b, 0, 0)))
        vt_out = (jax.ShapeDtypeStruct((bsz, LANES, n_tok), BF16),
                  pl.BlockSpec((None, LANES, tile), lambda i: (i // per_b, 0, i % per_b)))
        outs = [_row_out(rows, 512, BF16, tile), _row_out(rows, 512, F32, tile), _row_out(rows, 512, BF16, tile),
                _row_out(rows, 512, F32, tile), qt_out, _row_out(rows, 256, BF16, tile), vt_out]
        return _inproj_call(functools.partial(_inproj_ab_kernel, rope=rope), x2d, mod4, g4, 0, row_of,
                            weights0, (cos, sin), outs, tile, name)

    *gla_l, glag_l, qt_l, k_l, vt_l = inproj0(lat, n_lat, tm, lat_row, True, "inproj0_lat")
    *gla_c, glag_c, qt_c, k_c, vt_c = inproj0(cx, n_ctx, n_ctx, ctx_row, False, "inproj0_ctx")
    r3 = lambda a, n: a.reshape(bsz, n, a.shape[-1])
    gn_gla = jnp.tile(gla_norm_g[0], 8).reshape(1, 512)
    triple = lambda parts, n: (r3(parts[0], n), r3(parts[2], n), r3(parts[1], n))
    gf_c, gb_c, gf_l, gb_l = _lin_mixer(triple(gla_c, n_ctx), triple(gla_l, n_lat), None, ctx_out=True,
                                        dk=32, hk=256, hv=512, q_scale=32 ** -0.5, name="gla")
    k_c3 = r3(k_c, n_ctx)
    yb_l = _attn_call(qt_l, k_c3, vt_c, r3(k_l, n_lat), vt_l, tk=512, with_lat=True, name="attn_lat")
    yb_c = _attn_call(qt_c, k_c3, vt_c, k_c3, vt_c, tk=n_ctx, with_lat=False, name="attn_ctx")

    r2 = lambda a: a.reshape(-1, a.shape[-1])

    def out_mlp(x2d, halves, layer, row_of, name):
        wo = w_out[layer].astype(BF16)
        return _out_mlp_call(x2d, halves, ones512, mod4, g4, layer, row_of, wo[:512], wo[512:],
                             mlp_w1[layer].astype(BF16), mlp_w2[layer].astype(BF16), tm, name)

    lat = out_mlp(lat, [("lin", [(r2(gf_l), 0), (r2(gb_l), 0), (glag_l, 0), gn_gla]), ("raw", [(r2(yb_l), 0)])],
                  0, lat_row, "out_mlp0_lat")
    cx = out_mlp(cx, [("lin", [(r2(gf_c), 0), (r2(gb_c), 0), (glag_c, 0), gn_gla]), ("raw", [(r2(yb_c), 0)])],
                 0, ctx_row, "out_mlp0_ctx")

    weights1 = [cd_w_in[0].astype(BF16)]

    def inproj1(x2d, row_of, rope, name):
        rows = x2d.shape[0]
        outs = [_row_out(rows, 1024, F32, tm), _row_out(rows, 1024, BF16, tm),
                _row_out(rows, 512, BF16, tm), _row_out(rows, 512, F32, tm)]
        return _inproj_call(functools.partial(_inproj_cd_kernel, rope=rope), x2d, mod4, g4, 1, row_of,
                            weights1, (cos, sin), outs, tm, name)

    ml_l, ret_l, retv_l, retg_l = inproj1(lat, lat_row, True, "inproj1_lat")
    ml_c, ret_c, retv_c, _ = inproj1(cx, ctx_row, False, "inproj1_ctx")

    wq, wk_m, wv_m = (_block_diag(ml_w_qkv[0, t]).astype(BF16) for t in range(3))
    wg_t = jnp.concatenate([ml_w_gate[0, 0].T, ml_w_gate[0, 1].T], axis=0).astype(BF16)
    bg_t = ml_b_gate[0].reshape(32, 1)
    prep = lambda a, n, name: _ml_prep_call(r3(a, n), ml_conv_w[0], ml_conv_b[0].reshape(1, 512),
                                            wq, wk_m, wv_m, wg_t, bg_t, name)
    xc_l, *ml_lat = prep(ml_l, n_lat, "ml_prep_lat")
    _, *ml_ctx = prep(ml_c, n_ctx, "ml_prep_ctx")
    head_of = jnp.arange(512) // HEAD_DIM
    expand2 = (jnp.arange(16)[:, None] == head_of[None, :]).astype(BF16)
    gn_ml = jnp.tile(ml_norm_g[0], 8).reshape(1, 512)
    hf, hb = _ml_mixer(ml_ctx, ml_lat, expand2, "mlstm")

    dec2 = jnp.repeat(ret_decay_logit[0], HEAD_DIM, axis=1)
    gn_ret = jnp.tile(ret_norm_g[0], 8).reshape(1, 512)
    _, _, rf, rb = _lin_mixer((r3(ret_c, n_ctx), r3(retv_c, n_ctx), None),
                              (r3(ret_l, n_lat), r3(retv_l, n_lat), None),
                              dec2, ctx_out=False, dk=64, hk=512, hv=512, q_scale=1.0, name="ret")

    lat = out_mlp(lat, [("ml", [(r2(hf), 0), (r2(hb), 0), (ml_l, 1), gn_ml, ml_skip[0].reshape(1, 512),
                                (r2(xc_l), 0)]),
                        ("lin", [(r2(rf), 0), (r2(rb), 0), (retg_l, 0), gn_ret])],
                  1, lat_row, "out_mlp1_lat")
    return lat.reshape(bsz, n_lat, d)
```
